```python
import math
import jax, jax.numpy as jnp
from jax import lax
import numpy as np

D_MODEL = 1024
BATCH = 8
SEQ = 4096
DEPTH = 1
DEC_BATCH = 2
DEC_SEQ = 16384
PAST_LEN = 128

GRID_W = 64
CONV_WIDTH = D_MODEL // 2
CONV_GROUPS = 8
N_HEADS = 8
HEAD_DIM = 64
ATTN_WIDTH = N_HEADS * HEAD_DIM
MIX_WIDTH = CONV_WIDTH + ATTN_WIDTH
WIN_ROWS = 8
WIN_COLS = 16
CONV_K = 3
PROJ_WIDTH = 4 * CONV_WIDTH + 4 * ATTN_WIDTH
DN_ALPHA = (2.0 * DEPTH) ** 0.25
DN_BETA = (8.0 * DEPTH) ** -0.25
NORM_EPS = 1e-5

kernel_name = "hymba_conv_natten_deepnorm_encoder"


def group_rmsnorm(y, gain, n_groups):
    b, l, w = y.shape
    y32 = y.astype(jnp.float32).reshape(b, l, n_groups, w // n_groups)
    y32 = y32 * lax.rsqrt(jnp.mean(y32 * y32, axis=-1, keepdims=True) + NORM_EPS)
    return (y32.reshape(b, l, w) * gain.astype(jnp.float32)).astype(y.dtype)


def layer_norm(x, g, b):
    x32 = x.astype(jnp.float32)
    mu = jnp.mean(x32, axis=-1, keepdims=True)
    var = jnp.mean(jnp.square(x32 - mu), axis=-1, keepdims=True)
    y = (x32 - mu) * lax.rsqrt(var + NORM_EPS) * g.astype(jnp.float32) + b.astype(jnp.float32)
    return y.astype(x.dtype)


def centred_depthwise_conv(u, w):
    up = jnp.pad(u, ((0, 0), (1, 1), (0, 0)))
    return up[:, :-2] * w[0] + up[:, 1:-1] * w[1] + up[:, 2:] * w[2]


def neighbourhood_attention(q, k, v, rpb):
    bsz, seqlen, h, hd = q.shape
    rows = seqlen // GRID_W
    kh = min(WIN_ROWS, rows)
    q = (q * (hd ** -0.5)).reshape(bsz, rows, GRID_W, h, hd)
    k = k.reshape(bsz, rows, GRID_W, h, hd)
    v = v.reshape(bsz, rows, GRID_W, h, hd)
    cols = np.arange(GRID_W)
    col_start = np.clip(cols - WIN_COLS // 2, 0, GRID_W - WIN_COLS)
    col_idx = col_start[:, None] + np.arange(WIN_COLS)[None, :]
    dc_idx = col_idx - cols[:, None] + (WIN_COLS - 1)
    col_bias = rpb[:, :, dc_idx]

    def row_block(r):
        rs = jnp.clip(r - kh // 2, 0, rows - kh)
        q_r = lax.dynamic_index_in_dim(q, r, axis=1, keepdims=False)
        k_band = lax.dynamic_slice_in_dim(k, rs, kh, axis=1)
        v_band = lax.dynamic_slice_in_dim(v, rs, kh, axis=1)
        k_win = k_band[:, :, col_idx]
        v_win = v_band[:, :, col_idx]
        dr_idx = rs + jnp.arange(kh) - r + (WIN_ROWS - 1)
        bias = jnp.take(col_bias, dr_idx, axis=1)
        bias = jnp.transpose(bias, (0, 2, 1, 3)).astype(jnp.float32)
        s = jnp.einsum('bchd,bkcjhd->bhckj', q_r, k_win).astype(jnp.float32) + bias[None]
        p = jax.nn.softmax(s.reshape(bsz, h, GRID_W, kh * WIN_COLS), axis=-1)
        p = p.reshape(bsz, h, GRID_W, kh, WIN_COLS).astype(v.dtype)
        return jnp.einsum('bhckj,bkcjhd->bchd', p, v_win)

    out = lax.map(row_block, jnp.arange(rows))
    return jnp.transpose(out, (1, 0, 2, 3, 4)).reshape(bsz, seqlen, h * hd)


def hybrid_layer(x, w_in, conv_w, rpb, g_conv, g_attn, w_out, ln_g, ln_b):
    bsz, seqlen, _ = x.shape
    proj = jnp.einsum('bld,dp->blp', x, w_in)
    splits = np.cumsum([CONV_WIDTH] * 4 + [ATTN_WIDTH] * 3)
    xc, bc, cc, zc, q, k, v, za = jnp.split(proj, splits, axis=-1)
    yc = bc * centred_depthwise_conv(cc * xc, conv_w)
    yc = group_rmsnorm(yc, g_conv, CONV_GROUPS) * jax.nn.silu(zc)
    shp = (bsz, seqlen, N_HEADS, HEAD_DIM)
    ya = neighbourhood_attention(q.reshape(shp), k.reshape(shp), v.reshape(shp), rpb)
    ya = group_rmsnorm(ya, g_attn, N_HEADS) * jax.nn.silu(za)
    mix = jnp.concatenate([yc, ya], axis=-1)
    out = jnp.einsum('blm,md->bld', mix, w_out)
    return layer_norm(DN_ALPHA * x + out, ln_g, ln_b)


def setup_inputs(seed: int = 0) -> dict:
    key = jax.random.key(seed)
    ks = jax.random.split(key, 12)
    x_prompt = jax.random.normal(ks[0], (BATCH, SEQ, D_MODEL), jnp.float32)
    x_sample = jax.random.normal(ks[1], (DEC_BATCH, DEC_SEQ, D_MODEL), jnp.float32)
    w_in = jax.random.normal(ks[2], (DEPTH, D_MODEL, PROJ_WIDTH), jnp.float32) * D_MODEL ** -0.5
    conv_w = jax.random.normal(ks[3], (DEPTH, CONV_K, CONV_WIDTH), jnp.float32) * CONV_K ** -0.5
    rpb = jax.random.normal(ks[4], (DEPTH, N_HEADS, 2 * WIN_ROWS - 1, 2 * WIN_COLS - 1), jnp.float32) * 0.02
    g_conv = 1.0 + 0.02 * jax.random.normal(ks[5], (DEPTH, CONV_WIDTH), jnp.float32)
    g_attn = 1.0 + 0.02 * jax.random.normal(ks[6], (DEPTH, ATTN_WIDTH), jnp.float32)
    w_out = jax.random.normal(ks[7], (DEPTH, MIX_WIDTH, D_MODEL), jnp.float32) * (MIX_WIDTH ** -0.5) * DN_BETA
    ln_g = 1.0 + 0.02 * jax.random.normal(ks[8], (DEPTH, D_MODEL), jnp.float32)
    ln_b = 0.02 * jax.random.normal(ks[9], (DEPTH, D_MODEL), jnp.float32)
    return {"x_prompt": x_prompt, "x_sample": x_sample, "w_in": w_in, "conv_w": conv_w, "rpb": rpb,
            "g_conv": g_conv, "g_attn": g_attn, "w_out": w_out, "ln_g": ln_g, "ln_b": ln_b}


def reference(x_prompt, x_sample, w_in, conv_w, rpb, g_conv, g_attn, w_out, ln_g, ln_b):
    y_prompt = x_prompt
    y_sample = x_sample
    for d in range(DEPTH):
        y_prompt = hybrid_layer(y_prompt, w_in[d], conv_w[d], rpb[d], g_conv[d], g_attn[d], w_out[d], ln_g[d], ln_b[d])
        y_sample = hybrid_layer(y_sample, w_in[d], conv_w[d], rpb[d], g_conv[d], g_attn[d], w_out[d], ln_g[d], ln_b[d])
    return (y_prompt, y_sample)
```

```python
import functools

import numpy as np
import jax
import jax.numpy as jnp
from jax import lax
from jax.experimental import pallas as pl
from jax.experimental.pallas import tpu as pltpu

D_MODEL = 1024
GRID_W = 64
CONV_WIDTH = 512
CONV_GROUPS = 8
N_HEADS = 8
HEAD_DIM = 64
ATTN_WIDTH = N_HEADS * HEAD_DIM
WIN_ROWS = 8
WIN_COLS = 16
NORM_EPS = 1e-5
DEPTH = 1
DN_ALPHA = (2.0 * DEPTH) ** 0.25

LANES = 128
GROUP_W = 64
PAIRS = ATTN_WIDTH // LANES
BAND = WIN_ROWS * GRID_W
HALO = 16

PROJ_TM = 512
ATTN_ROWS = 16
ATTN_HALO_ROWS = WIN_ROWS // 2
OUT_TM = 512
VMEM_LIMIT = 52 * 1024 * 1024


def _group_mean_square(y):
    rows, width = y.shape
    low = lax.broadcasted_iota(jnp.int32, (rows, LANES), 1) < GROUP_W
    out = []
    for c in range(width // LANES):
        sq = y[:, c * LANES:(c + 1) * LANES]
        sq = sq * sq
        s_lo = jnp.sum(jnp.where(low, sq, 0.0), axis=-1, keepdims=True)
        s_hi = jnp.sum(jnp.where(low, 0.0, sq), axis=-1, keepdims=True)
        out.append(jnp.where(low, s_lo, s_hi))
    return jnp.concatenate(out, axis=-1) * (1.0 / GROUP_W)


def _silu(z):
    return z * jax.nn.sigmoid(z)


def _proj_kernel(xm_ref, xp_ref, xn_ref, wuc_ref, wrest_ref, convw_ref, gconv_ref,
                 yc_ref, q_ref, k_ref, v_ref, ga_ref, u_ref):
    i = pl.program_id(1)
    n = pl.num_programs(1)
    tm = xm_ref.shape[1]
    cw = CONV_WIDTH

    xm = xm_ref[0].astype(jnp.bfloat16)
    xp = jnp.where(i > 0, xp_ref[0], 0.0).astype(jnp.bfloat16)
    xn = jnp.where(i < n - 1, xn_ref[0], 0.0).astype(jnp.bfloat16)
    xh = jnp.concatenate([xp, xm, xn], axis=0)

    xc_cc = jnp.dot(xh, wuc_ref[...], preferred_element_type=jnp.float32)
    u_ref[...] = xc_cc[:, :cw] * xc_cc[:, cw:]
    u_prev = u_ref[pl.ds(HALO - 1, tm), :]
    u_mid = u_ref[pl.ds(HALO, tm), :]
    u_next = u_ref[pl.ds(HALO + 1, tm), :]
    conv = u_prev * convw_ref[0:1, :] + u_mid * convw_ref[1:2, :] + u_next * convw_ref[2:3, :]

    def seg(s):
        return jnp.dot(xm, wrest_ref[:, s * cw:(s + 1) * cw], preferred_element_type=jnp.float32)

    y = seg(0) * conv
    y = y * lax.rsqrt(_group_mean_square(y) + NORM_EPS) * gconv_ref[...]
    yc_ref[0] = (y * _silu(seg(1))).astype(yc_ref.dtype)
    q_ref[0] = (seg(2) * (HEAD_DIM ** -0.5)).astype(q_ref.dtype)
    k_ref[0] = seg(3).astype(k_ref.dtype)
    v_ref[0] = seg(4).astype(v_ref.dtype)
    ga_ref[0] = _silu(seg(5)).astype(ga_ref.dtype)


def _proj_call(x, w_uc, w_rest, conv_w, g_conv):
    bsz, seqlen, d = x.shape
    tm = PROJ_TM
    nt = seqlen // tm
    hb = tm // HALO
    nhb = seqlen // HALO
    act = jax.ShapeDtypeStruct((bsz, seqlen, CONV_WIDTH), jnp.bfloat16)
    tile = pl.BlockSpec((1, tm, CONV_WIDTH), lambda b, i: (b, i, 0))
    return pl.pallas_call(
        _proj_kernel,
        grid=(bsz, nt),
        in_specs=[
            pl.BlockSpec((1, tm, d), lambda b, i: (b, i, 0)),
            pl.BlockSpec((1, HALO, d), lambda b, i: (b, jnp.maximum(i * hb - 1, 0), 0)),
            pl.BlockSpec((1, HALO, d), lambda b, i: (b, jnp.minimum((i + 1) * hb, nhb - 1), 0)),
            pl.BlockSpec(w_uc.shape, lambda b, i: (0, 0)),
            pl.BlockSpec(w_rest.shape, lambda b, i: (0, 0)),
            pl.BlockSpec(conv_w.shape, lambda b, i: (0, 0)),
            pl.BlockSpec(g_conv.shape, lambda b, i: (0, 0)),
        ],
        out_specs=[tile] * 5,
        out_shape=[act] * 5,
        scratch_shapes=[pltpu.VMEM((tm + 2 * HALO, CONV_WIDTH), jnp.float32)],
        compiler_params=pltpu.CompilerParams(
            dimension_semantics=("arbitrary", "arbitrary"), vmem_limit_bytes=VMEM_LIMIT),
        name="proj_conv",
    )(x, x, x, w_uc, w_rest, conv_w, g_conv)


def _attn_kernel(q_ref, kp_ref, kc_ref, kn_ref, vp_ref, vc_ref, vn_ref, ga_ref, bias_ref, gattn_ref,
                 o_ref, kbuf, vbuf, *, rows):
    i = pl.program_id(1)
    tq = q_ref.shape[1]
    halo = ATTN_HALO_ROWS * GRID_W
    r0 = i * ATTN_ROWS

    kbuf[0:halo, :] = kp_ref[0]
    kbuf[halo:halo + tq, :] = kc_ref[0]
    kbuf[halo + tq:, :] = kn_ref[0]
    vbuf[0:halo, :] = vp_ref[0]
    vbuf[halo:halo + tq, :] = vc_ref[0]
    vbuf[halo + tq:, :] = vn_ref[0]

    low = lax.broadcasted_iota(jnp.int32, (GRID_W, LANES), 1) < GROUP_W

    def row_body(rl, carry):
        r = r0 + rl
        rs = jnp.clip(r - WIN_ROWS // 2, 0, rows - WIN_ROWS)
        off = pl.multiple_of((rs - r0 + ATTN_HALO_ROWS) * GRID_W, GRID_W)
        variant = r - rs
        qoff = pl.multiple_of(rl * GRID_W, GRID_W)
        for p in range(PAIRS):
            lanes = slice(p * LANES, (p + 1) * LANES)
            qp = q_ref[0, pl.ds(qoff, GRID_W), lanes]
            zero = jnp.zeros_like(qp)
            qm = jnp.concatenate([jnp.where(low, qp, zero), jnp.where(low, zero, qp)], axis=0)
            kp = kbuf[pl.ds(off, BAND), lanes]
            s = lax.dot_general(qm, kp, (((1,), (1,)), ((), ())),
                                preferred_element_type=jnp.float32)
            s = s + bias_ref[variant, p]
            m = jnp.max(s, axis=-1, keepdims=True)
            e = jnp.exp(s - m)
            l = jnp.sum(e, axis=-1, keepdims=True)
            vp = vbuf[pl.ds(off, BAND), lanes]
            o2 = jnp.dot(e.astype(jnp.bfloat16), vp, preferred_element_type=jnp.float32)
            o2 = o2 / l
            o = jnp.where(low, o2[:GRID_W], o2[GRID_W:])
            o = o * lax.rsqrt(_group_mean_square(o) + NORM_EPS) * gattn_ref[:, lanes]
            g = ga_ref[0, pl.ds(qoff, GRID_W), lanes].astype(jnp.float32)
            o_ref[0, pl.ds(qoff, GRID_W), lanes] = (o * g).astype(o_ref.dtype)
        return carry

    lax.fori_loop(0, ATTN_ROWS, row_body, 0)


def _attn_call(q, k, v, ga, bias, g_attn):
    bsz, seqlen, w = q.shape
    rows = seqlen // GRID_W
    tq = ATTN_ROWS * GRID_W
    halo = ATTN_HALO_ROWS * GRID_W
    nc = rows // ATTN_ROWS
    hb = tq // halo
    nhb = seqlen // halo
    cur = pl.BlockSpec((1, tq, w), lambda b, i: (b, i, 0))
    prev = pl.BlockSpec((1, halo, w), lambda b, i: (b, jnp.maximum(i * hb - 1, 0), 0))
    nxt = pl.BlockSpec((1, halo, w), lambda b, i: (b, jnp.minimum((i + 1) * hb, nhb - 1), 0))
    return pl.pallas_call(
        functools.partial(_attn_kernel, rows=rows),
        grid=(bsz, nc),
        in_specs=[cur, prev, cur, nxt, prev, cur, nxt, cur,
                  pl.BlockSpec(bias.shape, lambda b, i: (0, 0, 0, 0)),
                  pl.BlockSpec(g_attn.shape, lambda b, i: (0, 0))],
        out_specs=cur,
        out_shape=jax.ShapeDtypeStruct((bsz, seqlen, w), jnp.bfloat16),
        scratch_shapes=[pltpu.VMEM((tq + 2 * halo, w), jnp.bfloat16),
                        pltpu.VMEM((tq + 2 * halo, w), jnp.bfloat16)],
        compiler_params=pltpu.CompilerParams(
            dimension_semantics=("arbitrary", "arbitrary"), vmem_limit_bytes=VMEM_LIMIT),
        name="natten",
    )(q, k, k, k, v, v, v, ga, bias, g_attn)


def _band_bias(rpb):
    cols = np.arange(GRID_W)
    col_start = np.clip(cols - WIN_COLS // 2, 0, GRID_W - WIN_COLS)
    kc = np.arange(GRID_W)
    valid = (kc[None, :] >= col_start[:, None]) & (kc[None, :] < col_start[:, None] + WIN_COLS)
    dc_idx = np.clip(kc[None, :] - cols[:, None] + (WIN_COLS - 1), 0, 2 * WIN_COLS - 2)
    var = np.arange(WIN_ROWS)
    j = np.arange(WIN_ROWS)
    dr_idx = j[None, :] - var[:, None] + (WIN_ROWS - 1)
    t = rpb[:, dr_idx[:, :, None, None], dc_idx[None, None, :, :]]
    t = jnp.where(valid[None, None, None], t, -jnp.inf)
    t = jnp.transpose(t, (1, 0, 3, 2, 4))
    return t.reshape(WIN_ROWS, PAIRS, 2 * GRID_W, BAND).astype(jnp.float32)


def _out_kernel(yc_ref, ya_ref, x_ref, wt_ref, wb_ref, g_ref, b_ref, o_ref):
    out = jnp.dot(yc_ref[0], wt_ref[...], preferred_element_type=jnp.float32)
    out = out + jnp.dot(ya_ref[0], wb_ref[...], preferred_element_type=jnp.float32)
    h = DN_ALPHA * x_ref[0] + out
    mu = jnp.mean(h, axis=-1, keepdims=True)
    hc = h - mu
    var = jnp.mean(hc * hc, axis=-1, keepdims=True)
    o_ref[0] = hc * lax.rsqrt(var + NORM_EPS) * g_ref[...] + b_ref[...]


def _out_call(yc, ya, x, w_top, w_bot, ln_g, ln_b):
    bsz, seqlen, d = x.shape
    tm = OUT_TM
    half = pl.BlockSpec((1, tm, CONV_WIDTH), lambda b, i: (b, i, 0))
    full = pl.BlockSpec((1, tm, d), lambda b, i: (b, i, 0))
    const = lambda a: pl.BlockSpec(a.shape, lambda b, i: (0, 0))
    return pl.pallas_call(
        _out_kernel,
        grid=(bsz, seqlen // tm),
        in_specs=[half, half, full, const(w_top), const(w_bot), const(ln_g), const(ln_b)],
        out_specs=full,
        out_shape=jax.ShapeDtypeStruct(x.shape, jnp.float32),
        compiler_params=pltpu.CompilerParams(
            dimension_semantics=("arbitrary", "arbitrary"), vmem_limit_bytes=VMEM_LIMIT),
        name="out_proj_ln",
    )(yc, ya, x, w_top, w_bot, ln_g, ln_b)


def _layer(x, w_in, conv_w, bias, g_conv, g_attn, w_out, ln_g, ln_b):
    cw = CONV_WIDTH
    w_in = w_in.astype(jnp.bfloat16)
    w_uc = jnp.concatenate([w_in[:, 0:cw], w_in[:, 2 * cw:3 * cw]], axis=1)
    w_rest = jnp.concatenate([w_in[:, cw:2 * cw], w_in[:, 3 * cw:]], axis=1)
    w_out = w_out.astype(jnp.bfloat16)
    yc, q, k, v, ga = _proj_call(x, w_uc, w_rest, conv_w, g_conv.reshape(1, -1))
    ya = _attn_call(q, k, v, ga, bias, g_attn.reshape(1, -1))
    return _out_call(yc, ya, x, w_out[:cw], w_out[cw:], ln_g.reshape(1, -1), ln_b.reshape(1, -1))


def kernel(x_prompt, x_sample, w_in, conv_w, rpb, g_conv, g_attn, w_out, ln_g, ln_b):
    y_prompt, y_sample = x_prompt, x_sample
    for d in range(DEPTH):
        bias = _band_bias(rpb[d])
        args = (w_in[d], conv_w[d], bias, g_conv[d], g_attn[d], w_out[d], ln_g[d], ln_b[d])
        y_prompt = _layer(y_prompt, *args)
        y_sample = _layer(y_sample, *args)
    return (y_prompt, y_sample)
```

```python
import functools

import numpy as np
import jax
import jax.numpy as jnp
from jax import lax
from jax.experimental import pallas as pl
from jax.experimental.pallas import tpu as pltpu

D_MODEL = 1024
GRID_W = 64
CONV_WIDTH = 512
CONV_GROUPS = 8
N_HEADS = 8
HEAD_DIM = 64
ATTN_WIDTH = N_HEADS * HEAD_DIM
WIN_ROWS = 8
WIN_COLS = 16
NORM_EPS = 1e-5
DEPTH = 1
DN_ALPHA = (2.0 * DEPTH) ** 0.25

LANES = 128
GROUP_W = 64
PAIRS = ATTN_WIDTH // LANES
BAND = WIN_ROWS * GRID_W
HALO = 16

PROJ_TM = 512
ATTN_ROWS = 32
ATTN_HALO_ROWS = WIN_ROWS // 2
GROUP_ROWS = 2
GROUP_TASKS = GROUP_ROWS * PAIRS
OUT_TM = 512
VMEM_LIMIT = 52 * 1024 * 1024


def _group_mean_square(y):
    rows, width = y.shape
    low = lax.broadcasted_iota(jnp.int32, (rows, LANES), 1) < GROUP_W
    out = []
    for c in range(width // LANES):
        sq = y[:, c * LANES:(c + 1) * LANES]
        sq = sq * sq
        s_lo = jnp.sum(jnp.where(low, sq, 0.0), axis=-1, keepdims=True)
        s_hi = jnp.sum(jnp.where(low, 0.0, sq), axis=-1, keepdims=True)
        out.append(jnp.where(low, s_lo, s_hi))
    return jnp.concatenate(out, axis=-1) * (1.0 / GROUP_W)


def _silu(z):
    return z * jax.nn.sigmoid(z)


def _proj_kernel(xm_ref, xp_ref, xn_ref, wuc_ref, wrest_ref, convw_ref, gconv_ref,
                 yc_ref, q_ref, k_ref, v_ref, ga_ref, u_ref):
    i = pl.program_id(1)
    n = pl.num_programs(1)
    tm = xm_ref.shape[1]
    cw = CONV_WIDTH

    xm = xm_ref[0].astype(jnp.bfloat16)
    xp = jnp.where(i > 0, xp_ref[0], 0.0).astype(jnp.bfloat16)
    xn = jnp.where(i < n - 1, xn_ref[0], 0.0).astype(jnp.bfloat16)
    xh = jnp.concatenate([xp, xm, xn], axis=0)

    xc_cc = jnp.dot(xh, wuc_ref[...], preferred_element_type=jnp.float32)
    u_ref[...] = xc_cc[:, :cw] * xc_cc[:, cw:]
    u_prev = u_ref[pl.ds(HALO - 1, tm), :]
    u_mid = u_ref[pl.ds(HALO, tm), :]
    u_next = u_ref[pl.ds(HALO + 1, tm), :]
    conv = u_prev * convw_ref[0:1, :] + u_mid * convw_ref[1:2, :] + u_next * convw_ref[2:3, :]

    def seg(s):
        return jnp.dot(xm, wrest_ref[:, s * cw:(s + 1) * cw], preferred_element_type=jnp.float32)

    y = seg(0) * conv
    y = y * lax.rsqrt(_group_mean_square(y) + NORM_EPS) * gconv_ref[...]
    yc_ref[0] = (y * _silu(seg(1))).astype(yc_ref.dtype)
    q_ref[0] = (seg(2) * (HEAD_DIM ** -0.5)).astype(q_ref.dtype)
    k_ref[0] = seg(3).astype(k_ref.dtype)
    v_ref[0] = seg(4).astype(v_ref.dtype)
    ga_ref[0] = _silu(seg(5)).astype(ga_ref.dtype)


def _proj_call(x, w_uc, w_rest, conv_w, g_conv):
    bsz, seqlen, d = x.shape
    tm = PROJ_TM
    nt = seqlen // tm
    hb = tm // HALO
    nhb = seqlen // HALO
    act = jax.ShapeDtypeStruct((bsz, seqlen, CONV_WIDTH), jnp.bfloat16)
    tile = pl.BlockSpec((1, tm, CONV_WIDTH), lambda b, i: (b, i, 0))
    return pl.pallas_call(
        _proj_kernel,
        grid=(bsz, nt),
        in_specs=[
            pl.BlockSpec((1, tm, d), lambda b, i: (b, i, 0)),
            pl.BlockSpec((1, HALO, d), lambda b, i: (b, jnp.maximum(i * hb - 1, 0), 0)),
            pl.BlockSpec((1, HALO, d), lambda b, i: (b, jnp.minimum((i + 1) * hb, nhb - 1), 0)),
            pl.BlockSpec(w_uc.shape, lambda b, i: (0, 0)),
            pl.BlockSpec(w_rest.shape, lambda b, i: (0, 0)),
            pl.BlockSpec(conv_w.shape, lambda b, i: (0, 0)),
            pl.BlockSpec(g_conv.shape, lambda b, i: (0, 0)),
        ],
        out_specs=[tile] * 5,
        out_shape=[act] * 5,
        scratch_shapes=[pltpu.VMEM((tm + 2 * HALO, CONV_WIDTH), jnp.float32)],
        compiler_params=pltpu.CompilerParams(
            dimension_semantics=("arbitrary", "arbitrary"), vmem_limit_bytes=VMEM_LIMIT),
        name="proj_conv",
    )(x, x, x, w_uc, w_rest, conv_w, g_conv)


def _attn_kernel(q_ref, kp_ref, kc_ref, kn_ref, vp_ref, vc_ref, vn_ref, ga_ref, bias_ref, gattn_ref,
                 o_ref, kbuf, vbuf, pbuf, *, rows):
    i = pl.program_id(1)
    tq = q_ref.shape[1]
    halo = ATTN_HALO_ROWS * GRID_W
    r0 = i * ATTN_ROWS

    kbuf[0:halo, :] = kp_ref[0]
    kbuf[halo:halo + tq, :] = kc_ref[0]
    kbuf[halo + tq:, :] = kn_ref[0]
    vbuf[0:halo, :] = vp_ref[0]
    vbuf[halo:halo + tq, :] = vc_ref[0]
    vbuf[halo + tq:, :] = vn_ref[0]

    low = lax.broadcasted_iota(jnp.int32, (GRID_W, LANES), 1) < GROUP_W
    ones = jnp.ones((BAND, LANES), jnp.bfloat16)

    def task(g, t):
        rl = g * GROUP_ROWS + t // PAIRS
        r = r0 + rl
        rs = jnp.clip(r - WIN_ROWS // 2, 0, rows - WIN_ROWS)
        off = pl.multiple_of((rs - r0 + ATTN_HALO_ROWS) * GRID_W, GRID_W)
        qoff = pl.multiple_of(rl * GRID_W, GRID_W)
        p = t % PAIRS
        return off, r - rs, qoff, p, slice(p * LANES, (p + 1) * LANES)

    def scores(g, t, slot):
        off, variant, qoff, p, lanes = task(g, t)
        qp = q_ref[0, pl.ds(qoff, GRID_W), lanes]
        zero = jnp.zeros_like(qp)
        qm = jnp.concatenate([jnp.where(low, qp, zero), jnp.where(low, zero, qp)], axis=0)
        kp = kbuf[pl.ds(off, BAND), lanes]
        s = lax.dot_general(qm, kp, (((1,), (1,)), ((), ())),
                            preferred_element_type=jnp.float32)
        s = s + bias_ref[variant, p]
        m = jnp.max(s, axis=-1, keepdims=True)
        pbuf[slot, t] = jnp.exp(s - m).astype(jnp.bfloat16)

    def outputs(g, t, slot):
        off, _, qoff, _, lanes = task(g, t)
        vp = jnp.concatenate([vbuf[pl.ds(off, BAND), lanes], ones], axis=1)
        o2 = jnp.dot(pbuf[slot, t], vp, preferred_element_type=jnp.float32)
        o2 = o2[:, :LANES] / o2[:, LANES:]
        o = jnp.where(low, o2[:GRID_W], o2[GRID_W:])
        o = o * lax.rsqrt(_group_mean_square(o) + NORM_EPS) * gattn_ref[:, lanes]
        gate = ga_ref[0, pl.ds(qoff, GRID_W), lanes].astype(jnp.float32)
        o_ref[0, pl.ds(qoff, GRID_W), lanes] = (o * gate).astype(o_ref.dtype)

    n_groups = ATTN_ROWS // GROUP_ROWS
    for t in range(GROUP_TASKS):
        scores(0, t, 0)

    def group_body(g, carry):
        slot = g % 2
        for t in range(GROUP_TASKS):
            scores(g, t, slot)
            outputs(g - 1, t, 1 - slot)
        return carry

    lax.fori_loop(1, n_groups, group_body, 0)
    for t in range(GROUP_TASKS):
        outputs(n_groups - 1, t, (n_groups - 1) % 2)


def _attn_call(q, k, v, ga, bias, g_attn):
    bsz, seqlen, w = q.shape
    rows = seqlen // GRID_W
    tq = ATTN_ROWS * GRID_W
    halo = ATTN_HALO_ROWS * GRID_W
    nc = rows // ATTN_ROWS
    hb = tq // halo
    nhb = seqlen // halo
    cur = pl.BlockSpec((1, tq, w), lambda b, i: (b, i, 0))
    prev = pl.BlockSpec((1, halo, w), lambda b, i: (b, jnp.maximum(i * hb - 1, 0), 0))
    nxt = pl.BlockSpec((1, halo, w), lambda b, i: (b, jnp.minimum((i + 1) * hb, nhb - 1), 0))
    return pl.pallas_call(
        functools.partial(_attn_kernel, rows=rows),
        grid=(bsz, nc),
        in_specs=[cur, prev, cur, nxt, prev, cur, nxt, cur,
                  pl.BlockSpec(bias.shape, lambda b, i: (0, 0, 0, 0), pipeline_mode=pl.Buffered(1)),
                  pl.BlockSpec(g_attn.shape, lambda b, i: (0, 0))],
        out_specs=cur,
        out_shape=jax.ShapeDtypeStruct((bsz, seqlen, w), jnp.bfloat16),
        scratch_shapes=[pltpu.VMEM((tq + 2 * halo, w), jnp.bfloat16),
                        pltpu.VMEM((tq + 2 * halo, w), jnp.bfloat16),
                        pltpu.VMEM((2, GROUP_TASKS, 2 * GRID_W, BAND), jnp.bfloat16)],
        compiler_params=pltpu.CompilerParams(
            dimension_semantics=("arbitrary", "arbitrary"), vmem_limit_bytes=VMEM_LIMIT),
        name="natten",
    )(q, k, k, k, v, v, v, ga, bias, g_attn)


def _band_bias(rpb):
    cols = np.arange(GRID_W)
    col_start = np.clip(cols - WIN_COLS // 2, 0, GRID_W - WIN_COLS)
    kc = np.arange(GRID_W)
    valid = (kc[None, :] >= col_start[:, None]) & (kc[None, :] < col_start[:, None] + WIN_COLS)
    dc_idx = kc[None, :] - cols[:, None] + (WIN_COLS - 1)
    onehot = (dc_idx[None] == np.arange(2 * WIN_COLS - 1)[:, None, None]) & valid[None]
    mask = np.where(valid, 0.0, -np.inf).astype(np.float32)
    by_variant = jnp.stack([rpb[:, WIN_ROWS - 1 - v:2 * WIN_ROWS - 1 - v, :] for v in range(WIN_ROWS)])
    t = jnp.einsum("vhjm,mck->vhcjk", by_variant.astype(jnp.float32), onehot.astype(np.float32),
                   precision=lax.Precision.HIGHEST)
    t = t + mask[None, None, :, None, :]
    return t.reshape(WIN_ROWS, PAIRS, 2 * GRID_W, BAND)


def _out_kernel(yc_ref, ya_ref, x_ref, wt_ref, wb_ref, g_ref, b_ref, o_ref):
    out = jnp.dot(yc_ref[0], wt_ref[...], preferred_element_type=jnp.float32)
    out = out + jnp.dot(ya_ref[0], wb_ref[...], preferred_element_type=jnp.float32)
    h = DN_ALPHA * x_ref[0] + out
    mu = jnp.mean(h, axis=-1, keepdims=True)
    hc = h - mu
    var = jnp.mean(hc * hc, axis=-1, keepdims=True)
    o_ref[0] = hc * lax.rsqrt(var + NORM_EPS) * g_ref[...] + b_ref[...]


def _out_call(yc, ya, x, w_top, w_bot, ln_g, ln_b):
    bsz, seqlen, d = x.shape
    tm = OUT_TM
    half = pl.BlockSpec((1, tm, CONV_WIDTH), lambda b, i: (b, i, 0))
    full = pl.BlockSpec((1, tm, d), lambda b, i: (b, i, 0))
    const = lambda a: pl.BlockSpec(a.shape, lambda b, i: (0, 0))
    return pl.pallas_call(
        _out_kernel,
        grid=(bsz, seqlen // tm),
        in_specs=[half, half, full, const(w_top), const(w_bot), const(ln_g), const(ln_b)],
        out_specs=full,
        out_shape=jax.ShapeDtypeStruct(x.shape, jnp.float32),
        compiler_params=pltpu.CompilerParams(
            dimension_semantics=("arbitrary", "arbitrary"), vmem_limit_bytes=VMEM_LIMIT),
        name="out_proj_ln",
    )(yc, ya, x, w_top, w_bot, ln_g, ln_b)


def _layer(x, w_in, conv_w, bias, g_conv, g_attn, w_out, ln_g, ln_b):
    cw = CONV_WIDTH
    w_in = w_in.astype(jnp.bfloat16)
    w_uc = jnp.concatenate([w_in[:, 0:cw], w_in[:, 2 * cw:3 * cw]], axis=1)
    w_rest = jnp.concatenate([w_in[:, cw:2 * cw], w_in[:, 3 * cw:]], axis=1)
    w_out = w_out.astype(jnp.bfloat16)
    yc, q, k, v, ga = _proj_call(x, w_uc, w_rest, conv_w, g_conv.reshape(1, -1))
    ya = _attn_call(q, k, v, ga, bias, g_attn.reshape(1, -1))
    return _out_call(yc, ya, x, w_out[:cw], w_out[cw:], ln_g.reshape(1, -1), ln_b.reshape(1, -1))


def kernel(x_prompt, x_sample, w_in, conv_w, rpb, g_conv, g_attn, w_out, ln_g, ln_b):
    y_prompt, y_sample = x_prompt, x_sample
    for d in range(DEPTH):
        bias = _band_bias(rpb[d])
        args = (w_in[d], conv_w[d], bias, g_conv[d], g_attn[d], w_out[d], ln_g[d], ln_b[d])
        y_prompt = _layer(y_prompt, *args)
        y_sample = _layer(y_sample, *args)
    return (y_prompt, y_sample)
```

```python
import functools

import numpy as np
import jax
import jax.numpy as jnp
from jax import lax
from jax.experimental import pallas as pl
from jax.experimental.pallas import tpu as pltpu

D_MODEL = 1024
GRID_W = 64
CONV_WIDTH = 512
CONV_GROUPS = 8
N_HEADS = 8
HEAD_DIM = 64
ATTN_WIDTH = N_HEADS * HEAD_DIM
WIN_ROWS = 8
WIN_COLS = 16
NORM_EPS = 1e-5
DEPTH = 1
DN_ALPHA = (2.0 * DEPTH) ** 0.25

LANES = 128
GROUP_W = 64
PAIRS = ATTN_WIDTH // LANES
BAND = WIN_ROWS * GRID_W
HALO = 16

PROJ_TM = 512
ATTN_ROWS = 32
ATTN_HALO_ROWS = WIN_ROWS // 2
GROUP_ROWS = 2
GROUP_TASKS = GROUP_ROWS * PAIRS
OUT_TM = 512
VMEM_LIMIT = 52 * 1024 * 1024


def _group_mean_square(y):
    rows, width = y.shape
    low = lax.broadcasted_iota(jnp.int32, (rows, LANES), 1) < GROUP_W
    out = []
    for c in range(width // LANES):
        sq = y[:, c * LANES:(c + 1) * LANES]
        sq = sq * sq
        s_lo = jnp.sum(jnp.where(low, sq, 0.0), axis=-1, keepdims=True)
        s_hi = jnp.sum(jnp.where(low, 0.0, sq), axis=-1, keepdims=True)
        out.append(jnp.where(low, s_lo, s_hi))
    return jnp.concatenate(out, axis=-1) * (1.0 / GROUP_W)


def _silu(z):
    return z * jax.nn.sigmoid(z)


def _proj_kernel(xm_ref, xp_ref, xn_ref, wuc_ref, wrest_ref, convw_ref, gconv_ref,
                 yc_ref, q_ref, k_ref, v_ref, ga_ref, u_ref):
    i = pl.program_id(1)
    n = pl.num_programs(1)
    tm = xm_ref.shape[1]
    cw = CONV_WIDTH

    xm = xm_ref[0].astype(jnp.bfloat16)
    xp = jnp.where(i > 0, xp_ref[0], 0.0).astype(jnp.bfloat16)
    xn = jnp.where(i < n - 1, xn_ref[0], 0.0).astype(jnp.bfloat16)
    xh = jnp.concatenate([xp, xm, xn], axis=0)

    xc_cc = jnp.dot(xh, wuc_ref[...], preferred_element_type=jnp.float32)
    u_ref[...] = xc_cc[:, :cw] * xc_cc[:, cw:]
    u_prev = u_ref[pl.ds(HALO - 1, tm), :]
    u_mid = u_ref[pl.ds(HALO, tm), :]
    u_next = u_ref[pl.ds(HALO + 1, tm), :]
    conv = u_prev * convw_ref[0:1, :] + u_mid * convw_ref[1:2, :] + u_next * convw_ref[2:3, :]

    def seg(s):
        return jnp.dot(xm, wrest_ref[:, s * cw:(s + 1) * cw], preferred_element_type=jnp.float32)

    y = seg(0) * conv
    y = y * lax.rsqrt(_group_mean_square(y) + NORM_EPS) * gconv_ref[...]
    yc_ref[0] = (y * _silu(seg(1))).astype(yc_ref.dtype)
    q_ref[0] = (seg(2) * (HEAD_DIM ** -0.5)).astype(q_ref.dtype)
    k_ref[0] = seg(3).astype(k_ref.dtype)
    v_ref[0] = seg(4).astype(v_ref.dtype)
    ga_ref[0] = _silu(seg(5)).astype(ga_ref.dtype)


def _proj_call(x, w_uc, w_rest, conv_w, g_conv):
    bsz, seqlen, d = x.shape
    tm = PROJ_TM
    nt = seqlen // tm
    hb = tm // HALO
    nhb = seqlen // HALO
    act = jax.ShapeDtypeStruct((bsz, seqlen, CONV_WIDTH), jnp.bfloat16)
    tile = pl.BlockSpec((1, tm, CONV_WIDTH), lambda b, i: (b, i, 0))
    return pl.pallas_call(
        _proj_kernel,
        grid=(bsz, nt),
        in_specs=[
            pl.BlockSpec((1, tm, d), lambda b, i: (b, i, 0)),
            pl.BlockSpec((1, HALO, d), lambda b, i: (b, jnp.maximum(i * hb - 1, 0), 0)),
            pl.BlockSpec((1, HALO, d), lambda b, i: (b, jnp.minimum((i + 1) * hb, nhb - 1), 0)),
            pl.BlockSpec(w_uc.shape, lambda b, i: (0, 0)),
            pl.BlockSpec(w_rest.shape, lambda b, i: (0, 0)),
            pl.BlockSpec(conv_w.shape, lambda b, i: (0, 0)),
            pl.BlockSpec(g_conv.shape, lambda b, i: (0, 0)),
        ],
        out_specs=[tile] * 5,
        out_shape=[act] * 5,
        scratch_shapes=[pltpu.VMEM((tm + 2 * HALO, CONV_WIDTH), jnp.float32)],
        compiler_params=pltpu.CompilerParams(
            dimension_semantics=("arbitrary", "arbitrary"), vmem_limit_bytes=VMEM_LIMIT),
        name="proj_conv",
    )(x, x, x, w_uc, w_rest, conv_w, g_conv)


def _attn_kernel(q_ref, kp_ref, kc_ref, kn_ref, vp_ref, vc_ref, vn_ref, ga_ref, bias_ref, gattn_ref,
                 o_ref, kbuf, vbuf, pbuf0, pbuf1, *, rows):
    i = pl.program_id(1)
    tq = q_ref.shape[1]
    halo = ATTN_HALO_ROWS * GRID_W
    r0 = i * ATTN_ROWS

    kbuf[0:halo, :] = kp_ref[0]
    kbuf[halo:halo + tq, :] = kc_ref[0]
    kbuf[halo + tq:, :] = kn_ref[0]
    vbuf[0:halo, :] = vp_ref[0]
    vbuf[halo:halo + tq, :] = vc_ref[0]
    vbuf[halo + tq:, :] = vn_ref[0]

    low = lax.broadcasted_iota(jnp.int32, (GRID_W, LANES), 1) < GROUP_W
    ones = jnp.ones((BAND, LANES), jnp.bfloat16)

    def task(g, t):
        rl = g * GROUP_ROWS + t // PAIRS
        r = r0 + rl
        rs = jnp.clip(r - WIN_ROWS // 2, 0, rows - WIN_ROWS)
        off = pl.multiple_of((rs - r0 + ATTN_HALO_ROWS) * GRID_W, GRID_W)
        qoff = pl.multiple_of(rl * GRID_W, GRID_W)
        p = t % PAIRS
        return off, r - rs, qoff, p, slice(p * LANES, (p + 1) * LANES)

    def scores(g, t, pbuf):
        off, variant, qoff, p, lanes = task(g, t)
        qp = q_ref[0, pl.ds(qoff, GRID_W), lanes]
        zero = jnp.zeros_like(qp)
        qm = jnp.concatenate([jnp.where(low, qp, zero), jnp.where(low, zero, qp)], axis=0)
        kp = kbuf[pl.ds(off, BAND), lanes]
        s = lax.dot_general(qm, kp, (((1,), (1,)), ((), ())),
                            preferred_element_type=jnp.float32)
        s = s + bias_ref[variant, p]
        m = jnp.max(s, axis=-1, keepdims=True)
        pbuf[t] = jnp.exp(s - m).astype(jnp.bfloat16)

    def outputs(g, t, pbuf):
        off, _, qoff, _, lanes = task(g, t)
        vp = jnp.concatenate([vbuf[pl.ds(off, BAND), lanes], ones], axis=1)
        o2 = jnp.dot(pbuf[t], vp, preferred_element_type=jnp.float32)
        o2 = o2[:, :LANES] / o2[:, LANES:]
        o = jnp.where(low, o2[:GRID_W], o2[GRID_W:])
        o = o * lax.rsqrt(_group_mean_square(o) + NORM_EPS) * gattn_ref[:, lanes]
        gate = ga_ref[0, pl.ds(qoff, GRID_W), lanes].astype(jnp.float32)
        o_ref[0, pl.ds(qoff, GRID_W), lanes] = (o * gate).astype(o_ref.dtype)

    n_groups = ATTN_ROWS // GROUP_ROWS

    def step(g, p_new, p_old):
        for t in range(GROUP_TASKS):
            scores(g, t, p_new)
            outputs(g - 1, t, p_old)

    for t in range(GROUP_TASKS):
        scores(0, t, pbuf0)

    def two_groups(k, carry):
        step(2 * k + 1, pbuf1, pbuf0)
        step(2 * k + 2, pbuf0, pbuf1)
        return carry

    lax.fori_loop(0, (n_groups - 2) // 2, two_groups, 0)
    step(n_groups - 1, pbuf1, pbuf0)
    for t in range(GROUP_TASKS):
        outputs(n_groups - 1, t, pbuf1)


def _attn_call(q, k, v, ga, bias, g_attn):
    bsz, seqlen, w = q.shape
    rows = seqlen // GRID_W
    tq = ATTN_ROWS * GRID_W
    halo = ATTN_HALO_ROWS * GRID_W
    nc = rows // ATTN_ROWS
    hb = tq // halo
    nhb = seqlen // halo
    cur = pl.BlockSpec((1, tq, w), lambda b, i: (b, i, 0))
    prev = pl.BlockSpec((1, halo, w), lambda b, i: (b, jnp.maximum(i * hb - 1, 0), 0))
    nxt = pl.BlockSpec((1, halo, w), lambda b, i: (b, jnp.minimum((i + 1) * hb, nhb - 1), 0))
    return pl.pallas_call(
        functools.partial(_attn_kernel, rows=rows),
        grid=(bsz, nc),
        in_specs=[cur, prev, cur, nxt, prev, cur, nxt, cur,
                  pl.BlockSpec(bias.shape, lambda b, i: (0, 0, 0, 0), pipeline_mode=pl.Buffered(1)),
                  pl.BlockSpec(g_attn.shape, lambda b, i: (0, 0))],
        out_specs=cur,
        out_shape=jax.ShapeDtypeStruct((bsz, seqlen, w), jnp.bfloat16),
        scratch_shapes=[pltpu.VMEM((tq + 2 * halo, w), jnp.bfloat16),
                        pltpu.VMEM((tq + 2 * halo, w), jnp.bfloat16),
                        pltpu.VMEM((GROUP_TASKS, 2 * GRID_W, BAND), jnp.bfloat16),
                        pltpu.VMEM((GROUP_TASKS, 2 * GRID_W, BAND), jnp.bfloat16)],
        compiler_params=pltpu.CompilerParams(
            dimension_semantics=("arbitrary", "arbitrary"), vmem_limit_bytes=VMEM_LIMIT),
        name="natten",
    )(q, k, k, k, v, v, v, ga, bias, g_attn)


def _band_bias(rpb):
    cols = np.arange(GRID_W)
    col_start = np.clip(cols - WIN_COLS // 2, 0, GRID_W - WIN_COLS)
    kc = np.arange(GRID_W)
    valid = (kc[None, :] >= col_start[:, None]) & (kc[None, :] < col_start[:, None] + WIN_COLS)
    dc_idx = kc[None, :] - cols[:, None] + (WIN_COLS - 1)
    onehot = (dc_idx[None] == np.arange(2 * WIN_COLS - 1)[:, None, None]) & valid[None]
    mask = np.where(valid, 0.0, -np.inf).astype(np.float32)
    by_variant = jnp.stack([rpb[:, WIN_ROWS - 1 - v:2 * WIN_ROWS - 1 - v, :] for v in range(WIN_ROWS)])
    t = jnp.einsum("vhjm,mck->vhcjk", by_variant.astype(jnp.float32), onehot.astype(np.float32),
                   precision=lax.Precision.HIGHEST)
    t = t + mask[None, None, :, None, :]
    return t.reshape(WIN_ROWS, PAIRS, 2 * GRID_W, BAND)


def _out_kernel(yc_ref, ya_ref, x_ref, wt_ref, wb_ref, g_ref, b_ref, o_ref):
    out = jnp.dot(yc_ref[0], wt_ref[...], preferred_element_type=jnp.float32)
    out = out + jnp.dot(ya_ref[0], wb_ref[...], preferred_element_type=jnp.float32)
    h = DN_ALPHA * x_ref[0] + out
    mu = jnp.mean(h, axis=-1, keepdims=True)
    hc = h - mu
    var = jnp.mean(hc * hc, axis=-1, keepdims=True)
    o_ref[0] = hc * lax.rsqrt(var + NORM_EPS) * g_ref[...] + b_ref[...]


def _out_call(yc, ya, x, w_top, w_bot, ln_g, ln_b):
    bsz, seqlen, d = x.shape
    tm = OUT_TM
    half = pl.BlockSpec((1, tm, CONV_WIDTH), lambda b, i: (b, i, 0))
    full = pl.BlockSpec((1, tm, d), lambda b, i: (b, i, 0))
    const = lambda a: pl.BlockSpec(a.shape, lambda b, i: (0, 0))
    return pl.pallas_call(
        _out_kernel,
        grid=(bsz, seqlen // tm),
        in_specs=[half, half, full, const(w_top), const(w_bot), const(ln_g), const(ln_b)],
        out_specs=full,
        out_shape=jax.ShapeDtypeStruct(x.shape, jnp.float32),
        compiler_params=pltpu.CompilerParams(
            dimension_semantics=("arbitrary", "arbitrary"), vmem_limit_bytes=VMEM_LIMIT),
        name="out_proj_ln",
    )(yc, ya, x, w_top, w_bot, ln_g, ln_b)


def _layer(x, w_in, conv_w, bias, g_conv, g_attn, w_out, ln_g, ln_b):
    cw = CONV_WIDTH
    w_in = w_in.astype(jnp.bfloat16)
    w_uc = jnp.concatenate([w_in[:, 0:cw], w_in[:, 2 * cw:3 * cw]], axis=1)
    w_rest = jnp.concatenate([w_in[:, cw:2 * cw], w_in[:, 3 * cw:]], axis=1)
    w_out = w_out.astype(jnp.bfloat16)
    yc, q, k, v, ga = _proj_call(x, w_uc, w_rest, conv_w, g_conv.reshape(1, -1))
    ya = _attn_call(q, k, v, ga, bias, g_attn.reshape(1, -1))
    return _out_call(yc, ya, x, w_out[:cw], w_out[cw:], ln_g.reshape(1, -1), ln_b.reshape(1, -1))


def kernel(x_prompt, x_sample, w_in, conv_w, rpb, g_conv, g_attn, w_out, ln_g, ln_b):
    y_prompt, y_sample = x_prompt, x_sample
    for d in range(DEPTH):
        bias = _band_bias(rpb[d])
        args = (w_in[d], conv_w[d], bias, g_conv[d], g_attn[d], w_out[d], ln_g[d], ln_b[d])
        y_prompt = _layer(y_prompt, *args)
        y_sample = _layer(y_sample, *args)
    return (y_prompt, y_sample)
```

```python
import functools

import numpy as np
import jax
import jax.numpy as jnp
from jax import lax
from jax.experimental import pallas as pl
from jax.experimental.pallas import tpu as pltpu

D_MODEL = 1024
GRID_W = 64
CONV_WIDTH = 512
CONV_GROUPS = 8
N_HEADS = 8
HEAD_DIM = 64
ATTN_WIDTH = N_HEADS * HEAD_DIM
WIN_ROWS = 8
WIN_COLS = 16
NORM_EPS = 1e-5
DEPTH = 1
DN_ALPHA = (2.0 * DEPTH) ** 0.25

LANES = 128
GROUP_W = 64
PAIRS = ATTN_WIDTH // LANES
BAND = WIN_ROWS * GRID_W
HALO = 16

CHUNK_ROWS = 16
CHUNK = CHUNK_ROWS * GRID_W
KV_HALO_ROWS = WIN_ROWS // 2
KV_HALO = KV_HALO_ROWS * GRID_W
PROJ_TM = 512
OUT_TM = 256
GROUP_ROWS = 2
GROUP_TASKS = GROUP_ROWS * PAIRS
VMEM_LIMIT = 58 * 1024 * 1024


def _group_mean_square(y):
    rows, width = y.shape
    low = lax.broadcasted_iota(jnp.int32, (rows, LANES), 1) < GROUP_W
    out = []
    for c in range(width // LANES):
        sq = y[:, c * LANES:(c + 1) * LANES]
        sq = sq * sq
        s_lo = jnp.sum(jnp.where(low, sq, 0.0), axis=-1, keepdims=True)
        s_hi = jnp.sum(jnp.where(low, 0.0, sq), axis=-1, keepdims=True)
        out.append(jnp.where(low, s_lo, s_hi))
    return jnp.concatenate(out, axis=-1) * (1.0 / GROUP_W)


def _silu(z):
    return z * jax.nn.sigmoid(z)


def _layer_kernel(xm_ref, xcp_ref, xcn_ref, xnr_ref, wuc_ref, wrest_ref, convw_ref, gconv_ref,
                  bias_ref, gattn_ref, wtop_ref, wbot_ref, lng_ref, lnb_ref,
                  o_ref,
                  xh_s, u_s, q_s, ga_s, yc_s, ya_s, kbuf, vbuf, pbuf0, pbuf1, *, rows):
    i = pl.program_id(1)
    n = pl.num_programs(1)
    cw = CONV_WIDTH
    r0 = i * CHUNK_ROWS

    @pl.when(i > 0)
    def _():
        kbuf[0:KV_HALO, :] = kbuf[CHUNK:CHUNK + KV_HALO, :]
        vbuf[0:KV_HALO, :] = vbuf[CHUNK:CHUNK + KV_HALO, :]

    xh_s[0:HALO, :] = jnp.where(i > 0, xcp_ref[0], 0.0).astype(jnp.bfloat16)
    xh_s[HALO:HALO + CHUNK, :] = xm_ref[0].astype(jnp.bfloat16)
    xh_s[HALO + CHUNK:, :] = jnp.where(i < n - 1, xcn_ref[0], 0.0).astype(jnp.bfloat16)

    for s in range(CHUNK // PROJ_TM):
        tok = pl.ds(s * PROJ_TM, PROJ_TM)
        xh = xh_s[pl.ds(s * PROJ_TM, PROJ_TM + 2 * HALO), :]
        xm = xh_s[pl.ds(HALO + s * PROJ_TM, PROJ_TM), :]
        xc_cc = jnp.dot(xh, wuc_ref[...], preferred_element_type=jnp.float32)
        u_s[...] = xc_cc[:, :cw] * xc_cc[:, cw:]
        u_prev = u_s[pl.ds(HALO - 1, PROJ_TM), :]
        u_mid = u_s[pl.ds(HALO, PROJ_TM), :]
        u_next = u_s[pl.ds(HALO + 1, PROJ_TM), :]
        conv = u_prev * convw_ref[0:1, :] + u_mid * convw_ref[1:2, :] + u_next * convw_ref[2:3, :]

        def seg(c, xm=xm):
            return jnp.dot(xm, wrest_ref[:, c * cw:(c + 1) * cw], preferred_element_type=jnp.float32)

        y = seg(0) * conv
        y = y * lax.rsqrt(_group_mean_square(y) + NORM_EPS) * gconv_ref[...]
        yc_s[tok, :] = (y * _silu(seg(1))).astype(jnp.bfloat16)
        q_s[tok, :] = (seg(2) * (HEAD_DIM ** -0.5)).astype(jnp.bfloat16)
        kbuf[pl.ds(KV_HALO + s * PROJ_TM, PROJ_TM), :] = seg(3).astype(jnp.bfloat16)
        vbuf[pl.ds(KV_HALO + s * PROJ_TM, PROJ_TM), :] = seg(4).astype(jnp.bfloat16)
        ga_s[tok, :] = _silu(seg(5)).astype(jnp.bfloat16)

    xb = xnr_ref[0].astype(jnp.bfloat16)
    kbuf[KV_HALO + CHUNK:, :] = jnp.dot(
        xb, wrest_ref[:, 3 * cw:4 * cw], preferred_element_type=jnp.float32).astype(jnp.bfloat16)
    vbuf[KV_HALO + CHUNK:, :] = jnp.dot(
        xb, wrest_ref[:, 4 * cw:5 * cw], preferred_element_type=jnp.float32).astype(jnp.bfloat16)

    low = lax.broadcasted_iota(jnp.int32, (GRID_W, LANES), 1) < GROUP_W
    ones = jnp.ones((BAND, LANES), jnp.bfloat16)

    def task(g, t):
        rl = g * GROUP_ROWS + t // PAIRS
        r = r0 + rl
        rs = jnp.clip(r - WIN_ROWS // 2, 0, rows - WIN_ROWS)
        off = pl.multiple_of((rs - r0 + KV_HALO_ROWS) * GRID_W, GRID_W)
        qoff = pl.multiple_of(rl * GRID_W, GRID_W)
        p = t % PAIRS
        return off, r - rs, qoff, p, slice(p * LANES, (p + 1) * LANES)

    def scores(g, t, pbuf):
        off, variant, qoff, p, lanes = task(g, t)
        qp = q_s[pl.ds(qoff, GRID_W), lanes]
        zero = jnp.zeros_like(qp)
        qm = jnp.concatenate([jnp.where(low, qp, zero), jnp.where(low, zero, qp)], axis=0)
        kp = kbuf[pl.ds(off, BAND), lanes]
        s = lax.dot_general(qm, kp, (((1,), (1,)), ((), ())),
                            preferred_element_type=jnp.float32)
        s = s + bias_ref[variant, p]
        m = jnp.max(s, axis=-1, keepdims=True)
        pbuf[t] = jnp.exp(s - m).astype(jnp.bfloat16)

    def outputs(g, t, pbuf):
        off, _, qoff, _, lanes = task(g, t)
        vp = jnp.concatenate([vbuf[pl.ds(off, BAND), lanes], ones], axis=1)
        o2 = jnp.dot(pbuf[t], vp, preferred_element_type=jnp.float32)
        o2 = o2[:, :LANES] / o2[:, LANES:]
        o = jnp.where(low, o2[:GRID_W], o2[GRID_W:])
        o = o * lax.rsqrt(_group_mean_square(o) + NORM_EPS) * gattn_ref[:, lanes]
        gate = ga_s[pl.ds(qoff, GRID_W), lanes].astype(jnp.float32)
        ya_s[pl.ds(qoff, GRID_W), lanes] = (o * gate).astype(jnp.bfloat16)

    n_groups = CHUNK_ROWS // GROUP_ROWS

    def step(g, p_new, p_old):
        for t in range(GROUP_TASKS):
            scores(g, t, p_new)
            outputs(g - 1, t, p_old)

    for t in range(GROUP_TASKS):
        scores(0, t, pbuf0)

    def two_groups(k, carry):
        step(2 * k + 1, pbuf1, pbuf0)
        step(2 * k + 2, pbuf0, pbuf1)
        return carry

    lax.fori_loop(0, (n_groups - 2) // 2, two_groups, 0)
    step(n_groups - 1, pbuf1, pbuf0)
    for t in range(GROUP_TASKS):
        outputs(n_groups - 1, t, pbuf1)

    for s in range(CHUNK // OUT_TM):
        tok = pl.ds(s * OUT_TM, OUT_TM)
        out = jnp.dot(yc_s[tok, :], wtop_ref[...], preferred_element_type=jnp.float32)
        out = out + jnp.dot(ya_s[tok, :], wbot_ref[...], preferred_element_type=jnp.float32)
        h = DN_ALPHA * xm_ref[0, tok, :] + out
        mu = jnp.mean(h, axis=-1, keepdims=True)
        hc = h - mu
        var = jnp.mean(hc * hc, axis=-1, keepdims=True)
        o_ref[0, tok, :] = hc * lax.rsqrt(var + NORM_EPS) * lng_ref[...] + lnb_ref[...]


def _layer_call(x, w_uc, w_rest, conv_w, g_conv, bias, g_attn, w_top, w_bot, ln_g, ln_b):
    bsz, seqlen, d = x.shape
    rows = seqlen // GRID_W
    assert rows % CHUNK_ROWS == 0 and (CHUNK_ROWS // GROUP_ROWS) % 2 == 0
    nc = rows // CHUNK_ROWS
    n_conv_halo = seqlen // HALO
    n_kv_halo = seqlen // KV_HALO

    def const(a):
        return pl.BlockSpec(a.shape, lambda b, i: (0,) * a.ndim, pipeline_mode=pl.Buffered(1))

    chunk = pl.BlockSpec((1, CHUNK, d), lambda b, i: (b, i, 0))
    return pl.pallas_call(
        functools.partial(_layer_kernel, rows=rows),
        grid=(bsz, nc),
        in_specs=[
            chunk,
            pl.BlockSpec((1, HALO, d), lambda b, i: (b, jnp.maximum(i * (CHUNK // HALO) - 1, 0), 0)),
            pl.BlockSpec((1, HALO, d),
                         lambda b, i: (b, jnp.minimum((i + 1) * (CHUNK // HALO), n_conv_halo - 1), 0)),
            pl.BlockSpec((1, KV_HALO, d),
                         lambda b, i: (b, jnp.minimum((i + 1) * (CHUNK // KV_HALO), n_kv_halo - 1), 0)),
            const(w_uc), const(w_rest), const(conv_w), const(g_conv),
            const(bias), const(g_attn), const(w_top), const(w_bot), const(ln_g), const(ln_b),
        ],
        out_specs=chunk,
        out_shape=jax.ShapeDtypeStruct(x.shape, jnp.float32),
        scratch_shapes=[
            pltpu.VMEM((CHUNK + 2 * HALO, d), jnp.bfloat16),
            pltpu.VMEM((PROJ_TM + 2 * HALO, CONV_WIDTH), jnp.float32),
            pltpu.VMEM((CHUNK, ATTN_WIDTH), jnp.bfloat16),
            pltpu.VMEM((CHUNK, ATTN_WIDTH), jnp.bfloat16),
            pltpu.VMEM((CHUNK, CONV_WIDTH), jnp.bfloat16),
            pltpu.VMEM((CHUNK, ATTN_WIDTH), jnp.bfloat16),
            pltpu.VMEM((CHUNK + 2 * KV_HALO, ATTN_WIDTH), jnp.bfloat16),
            pltpu.VMEM((CHUNK + 2 * KV_HALO, ATTN_WIDTH), jnp.bfloat16),
            pltpu.VMEM((GROUP_TASKS, 2 * GRID_W, BAND), jnp.bfloat16),
            pltpu.VMEM((GROUP_TASKS, 2 * GRID_W, BAND), jnp.bfloat16),
        ],
        compiler_params=pltpu.CompilerParams(
            dimension_semantics=("arbitrary", "arbitrary"), vmem_limit_bytes=VMEM_LIMIT),
        name="hybrid_layer",
    )(x, x, x, x, w_uc, w_rest, conv_w, g_conv, bias, g_attn, w_top, w_bot, ln_g, ln_b)


def _band_bias(rpb):
    cols = np.arange(GRID_W)
    col_start = np.clip(cols - WIN_COLS // 2, 0, GRID_W - WIN_COLS)
    kc = np.arange(GRID_W)
    valid = (kc[None, :] >= col_start[:, None]) & (kc[None, :] < col_start[:, None] + WIN_COLS)
    dc_idx = kc[None, :] - cols[:, None] + (WIN_COLS - 1)
    onehot = (dc_idx[None] == np.arange(2 * WIN_COLS - 1)[:, None, None]) & valid[None]
    mask = np.where(valid, 0.0, -np.inf).astype(np.float32)
    by_variant = jnp.stack([rpb[:, WIN_ROWS - 1 - v:2 * WIN_ROWS - 1 - v, :] for v in range(WIN_ROWS)])
    t = jnp.einsum("vhjm,mck->vhcjk", by_variant.astype(jnp.float32), onehot.astype(np.float32),
                   precision=lax.Precision.HIGHEST)
    t = t + mask[None, None, :, None, :]
    return t.reshape(WIN_ROWS, PAIRS, 2 * GRID_W, BAND)


def kernel(x_prompt, x_sample, w_in, conv_w, rpb, g_conv, g_attn, w_out, ln_g, ln_b):
    cw = CONV_WIDTH
    y_prompt, y_sample = x_prompt, x_sample
    for d in range(DEPTH):
        w = w_in[d].astype(jnp.bfloat16)
        w_uc = jnp.concatenate([w[:, 0:cw], w[:, 2 * cw:3 * cw]], axis=1)
        w_rest = jnp.concatenate([w[:, cw:2 * cw], w[:, 3 * cw:]], axis=1)
        wo = w_out[d].astype(jnp.bfloat16)
        args = (w_uc, w_rest, conv_w[d], g_conv[d].reshape(1, -1), _band_bias(rpb[d]),
                g_attn[d].reshape(1, -1), wo[:cw], wo[cw:], ln_g[d].reshape(1, -1), ln_b[d].reshape(1, -1))
        y_prompt = _layer_call(y_prompt, *args)
        y_sample = _layer_call(y_sample, *args)
    return (y_prompt, y_sample)
```

```python
import functools

import numpy as np
import jax
import jax.numpy as jnp
from jax import lax
from jax.experimental import pallas as pl
from jax.experimental.pallas import tpu as pltpu

D_MODEL = 1024
GRID_W = 64
CONV_WIDTH = 512
CONV_GROUPS = 8
N_HEADS = 8
HEAD_DIM = 64
ATTN_WIDTH = N_HEADS * HEAD_DIM
WIN_ROWS = 8
WIN_COLS = 16
NORM_EPS = 1e-5
DEPTH = 1
DN_ALPHA = (2.0 * DEPTH) ** 0.25
LOG2E = 1.4426950408889634

LANES = 128
GROUP_W = 64
PAIRS = ATTN_WIDTH // LANES
BAND = WIN_ROWS * GRID_W
HALO = 16

CHUNK_ROWS = 16
CHUNK = CHUNK_ROWS * GRID_W
KV_HALO_ROWS = WIN_ROWS // 2
KV_HALO = KV_HALO_ROWS * GRID_W
PROJ_TM = 512
OUT_TM = 256
GROUP_ROWS = 2
GROUP_TASKS = GROUP_ROWS * PAIRS
VMEM_LIMIT = 58 * 1024 * 1024


def _group_mean_square(y):
    rows, width = y.shape
    low = lax.broadcasted_iota(jnp.int32, (rows, LANES), 1) < GROUP_W
    out = []
    for c in range(width // LANES):
        sq = y[:, c * LANES:(c + 1) * LANES]
        sq = sq * sq
        s_lo = jnp.sum(jnp.where(low, sq, 0.0), axis=-1, keepdims=True)
        s_hi = jnp.sum(jnp.where(low, 0.0, sq), axis=-1, keepdims=True)
        out.append(jnp.where(low, s_lo, s_hi))
    return jnp.concatenate(out, axis=-1) * (1.0 / GROUP_W)


def _silu(z):
    return z * jax.nn.sigmoid(z)


def _layer_kernel(xm_ref, xcp_ref, xcn_ref, xnr_ref, wuc_ref, wseg_ref, convw_ref, gconv_ref,
                  bias_ref, gattn_ref, wtop_ref, wbot_ref, lng_ref, lnb_ref,
                  o_ref,
                  xh_s, u_s, q_s, ga_s, yc_s, ya_s, kbuf, vbuf, pbuf0, pbuf1, *, rows):
    i = pl.program_id(1)
    n = pl.num_programs(1)
    cw = CONV_WIDTH
    r0 = i * CHUNK_ROWS

    @pl.when(i > 0)
    def _():
        kbuf[:, 0:KV_HALO, :] = kbuf[:, CHUNK:CHUNK + KV_HALO, :]
        vbuf[:, 0:KV_HALO, :] = vbuf[:, CHUNK:CHUNK + KV_HALO, :]

    def store_pairs(buf, start, val):
        val = val.astype(jnp.bfloat16)
        for p in range(PAIRS):
            buf[p, pl.ds(start, val.shape[0]), :] = val[:, p * LANES:(p + 1) * LANES]

    xh_s[0:HALO, :] = jnp.where(i > 0, xcp_ref[0], 0.0).astype(jnp.bfloat16)
    xh_s[HALO:HALO + CHUNK, :] = xm_ref[0].astype(jnp.bfloat16)
    xh_s[HALO + CHUNK:, :] = jnp.where(i < n - 1, xcn_ref[0], 0.0).astype(jnp.bfloat16)

    for s in range(CHUNK // PROJ_TM):
        tok = pl.ds(s * PROJ_TM, PROJ_TM)
        xh = xh_s[pl.ds(s * PROJ_TM, PROJ_TM + 2 * HALO), :]
        xm = xh_s[pl.ds(HALO + s * PROJ_TM, PROJ_TM), :]
        xc_cc = jnp.dot(xh, wuc_ref[...], preferred_element_type=jnp.float32)
        u_s[...] = xc_cc[:, :cw] * xc_cc[:, cw:]
        u_prev = u_s[pl.ds(HALO - 1, PROJ_TM), :]
        u_mid = u_s[pl.ds(HALO, PROJ_TM), :]
        u_next = u_s[pl.ds(HALO + 1, PROJ_TM), :]
        conv = u_prev * convw_ref[0:1, :] + u_mid * convw_ref[1:2, :] + u_next * convw_ref[2:3, :]

        def seg(c, xm=xm):
            return jnp.dot(xm, wseg_ref[c], preferred_element_type=jnp.float32)

        y = seg(0) * conv
        y = y * lax.rsqrt(_group_mean_square(y) + NORM_EPS) * gconv_ref[...]
        yc_s[tok, :] = (y * _silu(seg(1))).astype(jnp.bfloat16)
        q_s[tok, :] = (seg(2) * (HEAD_DIM ** -0.5 * LOG2E)).astype(jnp.bfloat16)
        store_pairs(kbuf, KV_HALO + s * PROJ_TM, seg(3))
        store_pairs(vbuf, KV_HALO + s * PROJ_TM, seg(4))
        ga_s[tok, :] = _silu(seg(5)).astype(jnp.bfloat16)

    xb = xnr_ref[0].astype(jnp.bfloat16)
    store_pairs(kbuf, KV_HALO + CHUNK, jnp.dot(xb, wseg_ref[3], preferred_element_type=jnp.float32))
    store_pairs(vbuf, KV_HALO + CHUNK, jnp.dot(xb, wseg_ref[4], preferred_element_type=jnp.float32))

    low = lax.broadcasted_iota(jnp.int32, (GRID_W, LANES), 1) < GROUP_W
    ones = jnp.ones((BAND, LANES), jnp.bfloat16)

    def task(g, t):
        rl = g * GROUP_ROWS + t // PAIRS
        r = r0 + rl
        rs = jnp.clip(r - WIN_ROWS // 2, 0, rows - WIN_ROWS)
        off = pl.multiple_of((rs - r0 + KV_HALO_ROWS) * GRID_W, GRID_W)
        qoff = pl.multiple_of(rl * GRID_W, GRID_W)
        p = t % PAIRS
        return off, r - rs, qoff, p, slice(p * LANES, (p + 1) * LANES)

    def scores(g, t, pbuf):
        off, variant, qoff, p, lanes = task(g, t)
        qp = q_s[pl.ds(qoff, GRID_W), lanes]
        zero = jnp.zeros_like(qp)
        qm = jnp.concatenate([jnp.where(low, qp, zero), jnp.where(low, zero, qp)], axis=0)
        kp = kbuf[p, pl.ds(off, BAND), :]
        s = lax.dot_general(qm, kp, (((1,), (1,)), ((), ())),
                            preferred_element_type=jnp.float32)
        s = s + bias_ref[variant, p]
        m = jnp.max(s, axis=-1, keepdims=True)
        pbuf[t] = jnp.exp2(s - m).astype(jnp.bfloat16)

    def outputs(g, t, pbuf):
        off, _, qoff, p, lanes = task(g, t)
        vp = jnp.concatenate([vbuf[p, pl.ds(off, BAND), :], ones], axis=1)
        o2 = jnp.dot(pbuf[t], vp, preferred_element_type=jnp.float32)
        o2 = o2[:, :LANES] / o2[:, LANES:]
        o = jnp.where(low, o2[:GRID_W], o2[GRID_W:])
        o = o * lax.rsqrt(_group_mean_square(o) + NORM_EPS) * gattn_ref[:, lanes]
        gate = ga_s[pl.ds(qoff, GRID_W), lanes].astype(jnp.float32)
        ya_s[pl.ds(qoff, GRID_W), lanes] = (o * gate).astype(jnp.bfloat16)

    n_groups = CHUNK_ROWS // GROUP_ROWS

    def step(g, p_new, p_old):
        for t in range(GROUP_TASKS):
            scores(g, t, p_new)
            outputs(g - 1, t, p_old)

    for t in range(GROUP_TASKS):
        scores(0, t, pbuf0)

    def two_groups(k, carry):
        step(2 * k + 1, pbuf1, pbuf0)
        step(2 * k + 2, pbuf0, pbuf1)
        return carry

    lax.fori_loop(0, (n_groups - 2) // 2, two_groups, 0)
    step(n_groups - 1, pbuf1, pbuf0)
    for t in range(GROUP_TASKS):
        outputs(n_groups - 1, t, pbuf1)

    for s in range(CHUNK // OUT_TM):
        tok = pl.ds(s * OUT_TM, OUT_TM)
        out = jnp.dot(yc_s[tok, :], wtop_ref[...], preferred_element_type=jnp.float32)
        out = out + jnp.dot(ya_s[tok, :], wbot_ref[...], preferred_element_type=jnp.float32)
        h = DN_ALPHA * xm_ref[0, tok, :] + out
        mu = jnp.mean(h, axis=-1, keepdims=True)
        hc = h - mu
        var = jnp.mean(hc * hc, axis=-1, keepdims=True)
        o_ref[0, tok, :] = hc * lax.rsqrt(var + NORM_EPS) * lng_ref[...] + lnb_ref[...]


def _layer_call(x, w_uc, w_seg, conv_w, g_conv, bias, g_attn, w_top, w_bot, ln_g, ln_b):
    bsz, seqlen, d = x.shape
    rows = seqlen // GRID_W
    assert rows % CHUNK_ROWS == 0 and (CHUNK_ROWS // GROUP_ROWS) % 2 == 0
    nc = rows // CHUNK_ROWS
    n_conv_halo = seqlen // HALO
    n_kv_halo = seqlen // KV_HALO

    def const(a):
        return pl.BlockSpec(a.shape, lambda b, i: (0,) * a.ndim, pipeline_mode=pl.Buffered(1))

    chunk = pl.BlockSpec((1, CHUNK, d), lambda b, i: (b, i, 0))
    return pl.pallas_call(
        functools.partial(_layer_kernel, rows=rows),
        grid=(bsz, nc),
        in_specs=[
            chunk,
            pl.BlockSpec((1, HALO, d), lambda b, i: (b, jnp.maximum(i * (CHUNK // HALO) - 1, 0), 0)),
            pl.BlockSpec((1, HALO, d),
                         lambda b, i: (b, jnp.minimum((i + 1) * (CHUNK // HALO), n_conv_halo - 1), 0)),
            pl.BlockSpec((1, KV_HALO, d),
                         lambda b, i: (b, jnp.minimum((i + 1) * (CHUNK // KV_HALO), n_kv_halo - 1), 0)),
            const(w_uc), const(w_seg), const(conv_w), const(g_conv),
            const(bias), const(g_attn), const(w_top), const(w_bot), const(ln_g), const(ln_b),
        ],
        out_specs=chunk,
        out_shape=jax.ShapeDtypeStruct(x.shape, jnp.float32),
        scratch_shapes=[
            pltpu.VMEM((CHUNK + 2 * HALO, d), jnp.bfloat16),
            pltpu.VMEM((PROJ_TM + 2 * HALO, CONV_WIDTH), jnp.float32),
            pltpu.VMEM((CHUNK, ATTN_WIDTH), jnp.bfloat16),
            pltpu.VMEM((CHUNK, ATTN_WIDTH), jnp.bfloat16),
            pltpu.VMEM((CHUNK, CONV_WIDTH), jnp.bfloat16),
            pltpu.VMEM((CHUNK, ATTN_WIDTH), jnp.bfloat16),
            pltpu.VMEM((PAIRS, CHUNK + 2 * KV_HALO, LANES), jnp.bfloat16),
            pltpu.VMEM((PAIRS, CHUNK + 2 * KV_HALO, LANES), jnp.bfloat16),
            pltpu.VMEM((GROUP_TASKS, 2 * GRID_W, BAND), jnp.bfloat16),
            pltpu.VMEM((GROUP_TASKS, 2 * GRID_W, BAND), jnp.bfloat16),
        ],
        compiler_params=pltpu.CompilerParams(
            dimension_semantics=("arbitrary", "arbitrary"), vmem_limit_bytes=VMEM_LIMIT),
        name="hybrid_layer",
    )(x, x, x, x, w_uc, w_seg, conv_w, g_conv, bias, g_attn, w_top, w_bot, ln_g, ln_b)


def _band_bias(rpb):
    cols = np.arange(GRID_W)
    col_start = np.clip(cols - WIN_COLS // 2, 0, GRID_W - WIN_COLS)
    kc = np.arange(GRID_W)
    valid = (kc[None, :] >= col_start[:, None]) & (kc[None, :] < col_start[:, None] + WIN_COLS)
    dc_idx = kc[None, :] - cols[:, None] + (WIN_COLS - 1)
    onehot = (dc_idx[None] == np.arange(2 * WIN_COLS - 1)[:, None, None]) & valid[None]
    mask = np.where(valid, 0.0, -np.inf).astype(np.float32)
    by_variant = jnp.stack([rpb[:, WIN_ROWS - 1 - v:2 * WIN_ROWS - 1 - v, :] for v in range(WIN_ROWS)])
    t = jnp.einsum("vhjm,mck->vhcjk", by_variant.astype(jnp.float32), onehot.astype(np.float32),
                   precision=lax.Precision.HIGHEST)
    t = t * LOG2E + mask[None, None, :, None, :]
    return t.reshape(WIN_ROWS, PAIRS, 2 * GRID_W, BAND)


def kernel(x_prompt, x_sample, w_in, conv_w, rpb, g_conv, g_attn, w_out, ln_g, ln_b):
    cw = CONV_WIDTH
    y_prompt, y_sample = x_prompt, x_sample
    for d in range(DEPTH):
        w = w_in[d].astype(jnp.bfloat16)
        w_uc = jnp.concatenate([w[:, 0:cw], w[:, 2 * cw:3 * cw]], axis=1)
        w_seg = jnp.stack([w[:, c * cw:(c + 1) * cw] for c in (1, 3, 4, 5, 6, 7)])
        wo = w_out[d].astype(jnp.bfloat16)
        args = (w_uc, w_seg, conv_w[d], g_conv[d].reshape(1, -1), _band_bias(rpb[d]),
                g_attn[d].reshape(1, -1), wo[:cw], wo[cw:], ln_g[d].reshape(1, -1), ln_b[d].reshape(1, -1))
        y_prompt = _layer_call(y_prompt, *args)
        y_sample = _layer_call(y_sample, *args)
    return (y_prompt, y_sample)
```

```python
import functools

import numpy as np
import jax
import jax.numpy as jnp
from jax import lax
from jax.experimental import pallas as pl
from jax.experimental.pallas import tpu as pltpu

D_MODEL = 1024
GRID_W = 64
CONV_WIDTH = 512
CONV_GROUPS = 8
N_HEADS = 8
HEAD_DIM = 64
ATTN_WIDTH = N_HEADS * HEAD_DIM
WIN_ROWS = 8
WIN_COLS = 16
NORM_EPS = 1e-5
DEPTH = 1
DN_ALPHA = (2.0 * DEPTH) ** 0.25
LOG2E = 1.4426950408889634

LANES = 128
GROUP_W = 64
PAIRS = ATTN_WIDTH // LANES
BAND = WIN_ROWS * GRID_W
HALO = 16

CHUNK_ROWS = 16
CHUNK = CHUNK_ROWS * GRID_W
KV_HALO_ROWS = WIN_ROWS // 2
KV_HALO = KV_HALO_ROWS * GRID_W
PROJ_TM = 512
FILL_TM = 256
GROUP_ROWS = 2
GROUP_TASKS = GROUP_ROWS * PAIRS
VMEM_LIMIT = 58 * 1024 * 1024


def _group_mean_square(y):
    rows, width = y.shape
    low = lax.broadcasted_iota(jnp.int32, (rows, LANES), 1) < GROUP_W
    out = []
    for c in range(width // LANES):
        sq = y[:, c * LANES:(c + 1) * LANES]
        sq = sq * sq
        s_lo = jnp.sum(jnp.where(low, sq, 0.0), axis=-1, keepdims=True)
        s_hi = jnp.sum(jnp.where(low, 0.0, sq), axis=-1, keepdims=True)
        out.append(jnp.where(low, s_lo, s_hi))
    return jnp.concatenate(out, axis=-1) * (1.0 / GROUP_W)


def _silu(z):
    return z * jax.nn.sigmoid(z)


def _layer_kernel(xm_ref, xcp_ref, xcn_ref, xnr_ref, wuc_ref, wseg_ref, convw_ref, gconv_ref,
                  bias_ref, gattn_ref, wtop_ref, wbot_ref, lng_ref, lnb_ref,
                  o_ref,
                  xh_s, u_s, q_s, ga_s, yc_s, ya_s, kbuf, vbuf, pbuf0, pbuf1, *, rows):
    i = pl.program_id(1)
    n = pl.num_programs(1)
    cw = CONV_WIDTH
    r0 = i * CHUNK_ROWS

    @pl.when(i > 0)
    def _():
        kbuf[:, 0:KV_HALO, :] = kbuf[:, CHUNK:CHUNK + KV_HALO, :]
        vbuf[:, 0:KV_HALO, :] = vbuf[:, CHUNK:CHUNK + KV_HALO, :]

    def store_pairs(buf, start, val):
        val = val.astype(jnp.bfloat16)
        for p in range(PAIRS):
            buf[p, pl.ds(start, val.shape[0]), :] = val[:, p * LANES:(p + 1) * LANES]

    xh_s[0:HALO, :] = jnp.where(i > 0, xcp_ref[0], 0.0).astype(jnp.bfloat16)
    xh_s[HALO:HALO + CHUNK, :] = xm_ref[0].astype(jnp.bfloat16)
    xh_s[HALO + CHUNK:, :] = jnp.where(i < n - 1, xcn_ref[0], 0.0).astype(jnp.bfloat16)

    def u_piece(s):
        half = (CHUNK + 2 * HALO) // 2
        xc_cc = jnp.dot(xh_s[pl.ds(s * half, half), :], wuc_ref[...], preferred_element_type=jnp.float32)
        u_s[pl.ds(s * half, half), :] = xc_cc[:, :cw] * xc_cc[:, cw:]

    def seg(s, c):
        xm = xh_s[pl.ds(HALO + s * PROJ_TM, PROJ_TM), :]
        return jnp.dot(xm, wseg_ref[c], preferred_element_type=jnp.float32)

    def q_piece(s):
        q_s[pl.ds(s * PROJ_TM, PROJ_TM), :] = (seg(s, 2) * (HEAD_DIM ** -0.5 * LOG2E)).astype(jnp.bfloat16)

    def k_piece(s):
        store_pairs(kbuf, KV_HALO + s * PROJ_TM, seg(s, 3))

    def v_piece(s):
        store_pairs(vbuf, KV_HALO + s * PROJ_TM, seg(s, 4))

    def gate_piece(s):
        ga_s[pl.ds(s * PROJ_TM, PROJ_TM), :] = _silu(seg(s, 5)).astype(jnp.bfloat16)

    def halo_kv_piece(_):
        xb = xnr_ref[0].astype(jnp.bfloat16)
        store_pairs(kbuf, KV_HALO + CHUNK, jnp.dot(xb, wseg_ref[3], preferred_element_type=jnp.float32))
        store_pairs(vbuf, KV_HALO + CHUNK, jnp.dot(xb, wseg_ref[4], preferred_element_type=jnp.float32))

    def conv_piece(c):
        t0 = c * FILL_TM
        xm = xh_s[pl.ds(HALO + t0, FILL_TM), :]
        u_prev = u_s[pl.ds(HALO - 1 + t0, FILL_TM), :]
        u_mid = u_s[pl.ds(HALO + t0, FILL_TM), :]
        u_next = u_s[pl.ds(HALO + 1 + t0, FILL_TM), :]
        conv = u_prev * convw_ref[0:1, :] + u_mid * convw_ref[1:2, :] + u_next * convw_ref[2:3, :]
        y = jnp.dot(xm, wseg_ref[0], preferred_element_type=jnp.float32) * conv
        y = y * lax.rsqrt(_group_mean_square(y) + NORM_EPS) * gconv_ref[...]
        z = jnp.dot(xm, wseg_ref[1], preferred_element_type=jnp.float32)
        yc_s[pl.ds(t0, FILL_TM), :] = (y * _silu(z)).astype(jnp.bfloat16)

    def out_piece(c):
        tok = pl.ds(c * FILL_TM, FILL_TM)
        out = jnp.dot(yc_s[tok, :], wtop_ref[...], preferred_element_type=jnp.float32)
        out = out + jnp.dot(ya_s[tok, :], wbot_ref[...], preferred_element_type=jnp.float32)
        h = DN_ALPHA * xm_ref[0, tok, :] + out
        mu = jnp.mean(h, axis=-1, keepdims=True)
        hc = h - mu
        var = jnp.mean(hc * hc, axis=-1, keepdims=True)
        o_ref[0, tok, :] = hc * lax.rsqrt(var + NORM_EPS) * lng_ref[...] + lnb_ref[...]

    low = lax.broadcasted_iota(jnp.int32, (GRID_W, LANES), 1) < GROUP_W
    ones = jnp.ones((BAND, LANES), jnp.bfloat16)

    def task(g, t):
        rl = g * GROUP_ROWS + t // PAIRS
        r = r0 + rl
        rs = jnp.clip(r - WIN_ROWS // 2, 0, rows - WIN_ROWS)
        off = pl.multiple_of((rs - r0 + KV_HALO_ROWS) * GRID_W, GRID_W)
        qoff = pl.multiple_of(rl * GRID_W, GRID_W)
        p = t % PAIRS
        return off, r - rs, qoff, p, slice(p * LANES, (p + 1) * LANES)

    def scores(g, t, pbuf):
        off, variant, qoff, p, lanes = task(g, t)
        qp = q_s[pl.ds(qoff, GRID_W), lanes]
        zero = jnp.zeros_like(qp)
        qm = jnp.concatenate([jnp.where(low, qp, zero), jnp.where(low, zero, qp)], axis=0)
        kp = kbuf[p, pl.ds(off, BAND), :]
        s = lax.dot_general(qm, kp, (((1,), (1,)), ((), ())),
                            preferred_element_type=jnp.float32)
        s = s + bias_ref[variant, p]
        m = jnp.max(s, axis=-1, keepdims=True)
        pbuf[t] = jnp.exp2(s - m).astype(jnp.bfloat16)

    def outputs(g, t, pbuf):
        off, _, qoff, p, lanes = task(g, t)
        vp = jnp.concatenate([vbuf[p, pl.ds(off, BAND), :], ones], axis=1)
        o2 = jnp.dot(pbuf[t], vp, preferred_element_type=jnp.float32)
        o2 = o2[:, :LANES] / o2[:, LANES:]
        o = jnp.where(low, o2[:GRID_W], o2[GRID_W:])
        o = o * lax.rsqrt(_group_mean_square(o) + NORM_EPS) * gattn_ref[:, lanes]
        gate = ga_s[pl.ds(qoff, GRID_W), lanes].astype(jnp.float32)
        ya_s[pl.ds(qoff, GRID_W), lanes] = (o * gate).astype(jnp.bfloat16)

    n_groups = CHUNK_ROWS // GROUP_ROWS
    pbufs = (pbuf0, pbuf1)

    def group_step(g):
        for t in range(GROUP_TASKS):
            scores(g, t, pbufs[g % 2])
            if g > 0:
                outputs(g - 1, t, pbufs[(g - 1) % 2])

    program = (
        (u_piece, 0), (u_piece, 1),
        (q_piece, 0), (k_piece, 0), (v_piece, 0), (gate_piece, 0),
        (q_piece, 1), (k_piece, 1), (v_piece, 1), (gate_piece, 1),
        (halo_kv_piece, 0),
        (group_step, 0),
        (group_step, 1), (conv_piece, 0),
        (group_step, 2), (conv_piece, 1),
        (group_step, 3), (out_piece, 0),
        (group_step, 4), (conv_piece, 2),
        (group_step, 5), (out_piece, 1),
        (group_step, 6), (conv_piece, 3),
        (group_step, 7), (out_piece, 2),
    )
    for piece, c in program:
        piece(c)
    for t in range(GROUP_TASKS):
        outputs(n_groups - 1, t, pbufs[(n_groups - 1) % 2])
    out_piece(3)


def _layer_call(x, w_uc, w_seg, conv_w, g_conv, bias, g_attn, w_top, w_bot, ln_g, ln_b):
    bsz, seqlen, d = x.shape
    rows = seqlen // GRID_W
    assert rows % CHUNK_ROWS == 0 and CHUNK // FILL_TM == 4 and CHUNK_ROWS // GROUP_ROWS == 8
    nc = rows // CHUNK_ROWS
    n_conv_halo = seqlen // HALO
    n_kv_halo = seqlen // KV_HALO

    def const(a):
        return pl.BlockSpec(a.shape, lambda b, i: (0,) * a.ndim, pipeline_mode=pl.Buffered(1))

    chunk = pl.BlockSpec((1, CHUNK, d), lambda b, i: (b, i, 0))
    return pl.pallas_call(
        functools.partial(_layer_kernel, rows=rows),
        grid=(bsz, nc),
        in_specs=[
            chunk,
            pl.BlockSpec((1, HALO, d), lambda b, i: (b, jnp.maximum(i * (CHUNK // HALO) - 1, 0), 0)),
            pl.BlockSpec((1, HALO, d),
                         lambda b, i: (b, jnp.minimum((i + 1) * (CHUNK // HALO), n_conv_halo - 1), 0)),
            pl.BlockSpec((1, KV_HALO, d),
                         lambda b, i: (b, jnp.minimum((i + 1) * (CHUNK // KV_HALO), n_kv_halo - 1), 0)),
            const(w_uc), const(w_seg), const(conv_w), const(g_conv),
            const(bias), const(g_attn), const(w_top), const(w_bot), const(ln_g), const(ln_b),
        ],
        out_specs=chunk,
        out_shape=jax.ShapeDtypeStruct(x.shape, jnp.float32),
        scratch_shapes=[
            pltpu.VMEM((CHUNK + 2 * HALO, d), jnp.bfloat16),
            pltpu.VMEM((CHUNK + 2 * HALO, CONV_WIDTH), jnp.float32),
            pltpu.VMEM((CHUNK, ATTN_WIDTH), jnp.bfloat16),
            pltpu.VMEM((CHUNK, ATTN_WIDTH), jnp.bfloat16),
            pltpu.VMEM((CHUNK, CONV_WIDTH), jnp.bfloat16),
            pltpu.VMEM((CHUNK, ATTN_WIDTH), jnp.bfloat16),
            pltpu.VMEM((PAIRS, CHUNK + 2 * KV_HALO, LANES), jnp.bfloat16),
            pltpu.VMEM((PAIRS, CHUNK + 2 * KV_HALO, LANES), jnp.bfloat16),
            pltpu.VMEM((GROUP_TASKS, 2 * GRID_W, BAND), jnp.bfloat16),
            pltpu.VMEM((GROUP_TASKS, 2 * GRID_W, BAND), jnp.bfloat16),
        ],
        compiler_params=pltpu.CompilerParams(
            dimension_semantics=("arbitrary", "arbitrary"), vmem_limit_bytes=VMEM_LIMIT),
        name="hybrid_layer",
    )(x, x, x, x, w_uc, w_seg, conv_w, g_conv, bias, g_attn, w_top, w_bot, ln_g, ln_b)


def _band_bias(rpb):
    cols = np.arange(GRID_W)
    col_start = np.clip(cols - WIN_COLS // 2, 0, GRID_W - WIN_COLS)
    kc = np.arange(GRID_W)
    valid = (kc[None, :] >= col_start[:, None]) & (kc[None, :] < col_start[:, None] + WIN_COLS)
    dc_idx = kc[None, :] - cols[:, None] + (WIN_COLS - 1)
    onehot = (dc_idx[None] == np.arange(2 * WIN_COLS - 1)[:, None, None]) & valid[None]
    mask = np.where(valid, 0.0, -np.inf).astype(np.float32)
    by_variant = jnp.stack([rpb[:, WIN_ROWS - 1 - v:2 * WIN_ROWS - 1 - v, :] for v in range(WIN_ROWS)])
    t = jnp.einsum("vhjm,mck->vhcjk", by_variant.astype(jnp.float32), onehot.astype(np.float32),
                   precision=lax.Precision.HIGHEST)
    t = t * LOG2E + mask[None, None, :, None, :]
    return t.reshape(WIN_ROWS, PAIRS, 2 * GRID_W, BAND)


def kernel(x_prompt, x_sample, w_in, conv_w, rpb, g_conv, g_attn, w_out, ln_g, ln_b):
    cw = CONV_WIDTH
    y_prompt, y_sample = x_prompt, x_sample
    for d in range(DEPTH):
        w = w_in[d].astype(jnp.bfloat16)
        w_uc = jnp.concatenate([w[:, 0:cw], w[:, 2 * cw:3 * cw]], axis=1)
        w_seg = jnp.stack([w[:, c * cw:(c + 1) * cw] for c in (1, 3, 4, 5, 6, 7)])
        wo = w_out[d].astype(jnp.bfloat16)
        args = (w_uc, w_seg, conv_w[d], g_conv[d].reshape(1, -1), _band_bias(rpb[d]),
                g_attn[d].reshape(1, -1), wo[:cw], wo[cw:], ln_g[d].reshape(1, -1), ln_b[d].reshape(1, -1))
        y_prompt = _layer_call(y_prompt, *args)
        y_sample = _layer_call(y_sample, *args)
    return (y_prompt, y_sample)
```

```python
import functools

import numpy as np
import jax
import jax.numpy as jnp
from jax import lax
from jax.experimental import pallas as pl
from jax.experimental.pallas import tpu as pltpu

D_MODEL = 1024
GRID_W = 64
CONV_WIDTH = 512
CONV_GROUPS = 8
N_HEADS = 8
HEAD_DIM = 64
ATTN_WIDTH = N_HEADS * HEAD_DIM
WIN_ROWS = 8
WIN_COLS = 16
NORM_EPS = 1e-5
DEPTH = 1
DN_ALPHA = (2.0 * DEPTH) ** 0.25
LOG2E = 1.4426950408889634

LANES = 128
GROUP_W = 64
PAIRS = ATTN_WIDTH // LANES
BAND = WIN_ROWS * GRID_W
HALO = 16

CHUNK_ROWS = 16
CHUNK = CHUNK_ROWS * GRID_W
KV_HALO_ROWS = WIN_ROWS // 2
KV_HALO = KV_HALO_ROWS * GRID_W
PROJ_TM = 512
FILL_TM = 256
GROUP_ROWS = 2
GROUP_TASKS = GROUP_ROWS * PAIRS
VMEM_LIMIT = 58 * 1024 * 1024


def _group_mean_square(y):
    rows, width = y.shape
    low = lax.broadcasted_iota(jnp.int32, (rows, LANES), 1) < GROUP_W
    out = []
    for c in range(width // LANES):
        sq = y[:, c * LANES:(c + 1) * LANES]
        sq = sq * sq
        s_lo = jnp.sum(jnp.where(low, sq, 0.0), axis=-1, keepdims=True)
        s_hi = jnp.sum(jnp.where(low, 0.0, sq), axis=-1, keepdims=True)
        out.append(jnp.where(low, s_lo, s_hi))
    return jnp.concatenate(out, axis=-1) * (1.0 / GROUP_W)


def _silu(z):
    return z * jax.nn.sigmoid(z)


def _layer_kernel(xm_ref, xcp_ref, xcn_ref, xnr_ref, wuc_ref, wseg_ref, convw_ref, gconv_ref,
                  bias_ref, gattn_ref, wtop_ref, wbot_ref, lng_ref, lnb_ref,
                  o_ref,
                  xh_s, u_s, q_s, ga_s, yc_s, ya_s, kbuf, vbuf, pbuf0, pbuf1, *, rows):
    i = pl.program_id(1)
    n = pl.num_programs(1)
    cw = CONV_WIDTH
    r0 = i * CHUNK_ROWS

    @pl.when(i > 0)
    def _():
        kbuf[:, 0:KV_HALO, :] = kbuf[:, CHUNK:CHUNK + KV_HALO, :]
        vbuf[:, 0:KV_HALO, :] = vbuf[:, CHUNK:CHUNK + KV_HALO, :]

    def store_pairs(buf, start, val):
        val = val.astype(jnp.bfloat16)
        for p in range(PAIRS):
            buf[p, pl.ds(start, val.shape[0]), :] = val[:, p * LANES:(p + 1) * LANES]

    xh_s[0:HALO, :] = jnp.where(i > 0, xcp_ref[0], 0.0).astype(jnp.bfloat16)
    xh_s[HALO:HALO + CHUNK, :] = xm_ref[0].astype(jnp.bfloat16)
    xh_s[HALO + CHUNK:, :] = jnp.where(i < n - 1, xcn_ref[0], 0.0).astype(jnp.bfloat16)

    def u_piece(s):
        half = (CHUNK + 2 * HALO) // 2
        xc_cc = jnp.dot(xh_s[pl.ds(s * half, half), :], wuc_ref[...], preferred_element_type=jnp.float32)
        u_s[pl.ds(s * half, half), :] = xc_cc[:, :cw] * xc_cc[:, cw:]

    def seg(s, c):
        xm = xh_s[pl.ds(HALO + s * PROJ_TM, PROJ_TM), :]
        return jnp.dot(xm, wseg_ref[c], preferred_element_type=jnp.float32)

    def q_piece(s):
        q_s[pl.ds(s * PROJ_TM, PROJ_TM), :] = (seg(s, 2) * (HEAD_DIM ** -0.5 * LOG2E)).astype(jnp.bfloat16)

    def k_piece(s):
        store_pairs(kbuf, KV_HALO + s * PROJ_TM, seg(s, 3))

    def v_piece(s):
        store_pairs(vbuf, KV_HALO + s * PROJ_TM, seg(s, 4))

    def gate_piece(s):
        ga_s[pl.ds(s * PROJ_TM, PROJ_TM), :] = _silu(seg(s, 5)).astype(jnp.bfloat16)

    def halo_kv_piece(_):
        xb = xnr_ref[0].astype(jnp.bfloat16)
        store_pairs(kbuf, KV_HALO + CHUNK, jnp.dot(xb, wseg_ref[3], preferred_element_type=jnp.float32))
        store_pairs(vbuf, KV_HALO + CHUNK, jnp.dot(xb, wseg_ref[4], preferred_element_type=jnp.float32))

    def conv_piece(c):
        t0 = c * FILL_TM
        xm = xh_s[pl.ds(HALO + t0, FILL_TM), :]
        ext = FILL_TM + 2 * HALO
        u_ext = u_s[pl.ds(t0, ext), :]
        u_prev = pltpu.roll(u_ext, 1, 0)[HALO:HALO + FILL_TM]
        u_mid = u_ext[HALO:HALO + FILL_TM]
        u_next = pltpu.roll(u_ext, ext - 1, 0)[HALO:HALO + FILL_TM]
        conv = u_prev * convw_ref[0:1, :] + u_mid * convw_ref[1:2, :] + u_next * convw_ref[2:3, :]
        y = jnp.dot(xm, wseg_ref[0], preferred_element_type=jnp.float32) * conv
        y = y * lax.rsqrt(_group_mean_square(y) + NORM_EPS) * gconv_ref[...]
        z = jnp.dot(xm, wseg_ref[1], preferred_element_type=jnp.float32)
        yc_s[pl.ds(t0, FILL_TM), :] = (y * _silu(z)).astype(jnp.bfloat16)

    def out_piece(c):
        tok = pl.ds(c * FILL_TM, FILL_TM)
        out = jnp.dot(yc_s[tok, :], wtop_ref[...], preferred_element_type=jnp.float32)
        out = out + jnp.dot(ya_s[tok, :], wbot_ref[...], preferred_element_type=jnp.float32)
        h = DN_ALPHA * xm_ref[0, tok, :] + out
        mu = jnp.mean(h, axis=-1, keepdims=True)
        hc = h - mu
        var = jnp.mean(hc * hc, axis=-1, keepdims=True)
        o_ref[0, tok, :] = hc * lax.rsqrt(var + NORM_EPS) * lng_ref[...] + lnb_ref[...]

    low = lax.broadcasted_iota(jnp.int32, (GRID_W, LANES), 1) < GROUP_W
    ones = jnp.ones((BAND, LANES), jnp.bfloat16)

    def task(g, t):
        rl = g * GROUP_ROWS + t // PAIRS
        r = r0 + rl
        rs = jnp.clip(r - WIN_ROWS // 2, 0, rows - WIN_ROWS)
        off = pl.multiple_of((rs - r0 + KV_HALO_ROWS) * GRID_W, GRID_W)
        qoff = pl.multiple_of(rl * GRID_W, GRID_W)
        p = t % PAIRS
        return off, r - rs, qoff, p, slice(p * LANES, (p + 1) * LANES)

    def scores(g, t, pbuf):
        off, variant, qoff, p, lanes = task(g, t)
        qp = q_s[pl.ds(qoff, GRID_W), lanes]
        zero = jnp.zeros_like(qp)
        qm = jnp.concatenate([jnp.where(low, qp, zero), jnp.where(low, zero, qp)], axis=0)
        kp = kbuf[p, pl.ds(off, BAND), :]
        s = lax.dot_general(qm, kp, (((1,), (1,)), ((), ())),
                            preferred_element_type=jnp.float32)
        bias = jnp.concatenate(
            [jnp.concatenate([bias_ref[variant, p, a, j2] for j2 in range(WIN_ROWS // 2)], axis=1)
             for a in range(2)], axis=0)
        s = s + bias
        m = jnp.max(s, axis=-1, keepdims=True)
        pbuf[t] = jnp.exp2(s - m).astype(jnp.bfloat16)

    def outputs(g, t, pbuf):
        off, _, qoff, p, lanes = task(g, t)
        vp = jnp.concatenate([vbuf[p, pl.ds(off, BAND), :], ones], axis=1)
        o2 = jnp.dot(pbuf[t], vp, preferred_element_type=jnp.float32)
        o2 = o2[:, :LANES] / o2[:, LANES:]
        o = jnp.where(low, o2[:GRID_W], o2[GRID_W:])
        o = o * lax.rsqrt(_group_mean_square(o) + NORM_EPS) * gattn_ref[:, lanes]
        gate = ga_s[pl.ds(qoff, GRID_W), lanes].astype(jnp.float32)
        ya_s[pl.ds(qoff, GRID_W), lanes] = (o * gate).astype(jnp.bfloat16)

    n_groups = CHUNK_ROWS // GROUP_ROWS
    pbufs = (pbuf0, pbuf1)

    def group_step(g):
        for t in range(GROUP_TASKS):
            scores(g, t, pbufs[g % 2])
            if g > 0:
                outputs(g - 1, t, pbufs[(g - 1) % 2])

    program = (
        (u_piece, 0), (u_piece, 1),
        (q_piece, 0), (k_piece, 0), (v_piece, 0), (gate_piece, 0),
        (q_piece, 1), (k_piece, 1), (v_piece, 1), (gate_piece, 1),
        (halo_kv_piece, 0),
        (group_step, 0),
        (group_step, 1), (conv_piece, 0),
        (group_step, 2), (conv_piece, 1),
        (group_step, 3), (out_piece, 0),
        (group_step, 4), (conv_piece, 2),
        (group_step, 5), (out_piece, 1),
        (group_step, 6), (conv_piece, 3),
        (group_step, 7), (out_piece, 2),
    )
    for piece, c in program:
        piece(c)
    for t in range(GROUP_TASKS):
        outputs(n_groups - 1, t, pbufs[(n_groups - 1) % 2])
    out_piece(3)


def _layer_call(x, w_uc, w_seg, conv_w, g_conv, bias, g_attn, w_top, w_bot, ln_g, ln_b):
    bsz, seqlen, d = x.shape
    rows = seqlen // GRID_W
    assert rows % CHUNK_ROWS == 0 and CHUNK // FILL_TM == 4 and CHUNK_ROWS // GROUP_ROWS == 8
    nc = rows // CHUNK_ROWS
    n_conv_halo = seqlen // HALO
    n_kv_halo = seqlen // KV_HALO

    def const(a):
        return pl.BlockSpec(a.shape, lambda b, i: (0,) * a.ndim, pipeline_mode=pl.Buffered(1))

    chunk = pl.BlockSpec((1, CHUNK, d), lambda b, i: (b, i, 0))
    return pl.pallas_call(
        functools.partial(_layer_kernel, rows=rows),
        grid=(bsz, nc),
        in_specs=[
            chunk,
            pl.BlockSpec((1, HALO, d), lambda b, i: (b, jnp.maximum(i * (CHUNK // HALO) - 1, 0), 0)),
            pl.BlockSpec((1, HALO, d),
                         lambda b, i: (b, jnp.minimum((i + 1) * (CHUNK // HALO), n_conv_halo - 1), 0)),
            pl.BlockSpec((1, KV_HALO, d),
                         lambda b, i: (b, jnp.minimum((i + 1) * (CHUNK // KV_HALO), n_kv_halo - 1), 0)),
            const(w_uc), const(w_seg), const(conv_w), const(g_conv),
            const(bias), const(g_attn), const(w_top), const(w_bot), const(ln_g), const(ln_b),
        ],
        out_specs=chunk,
        out_shape=jax.ShapeDtypeStruct(x.shape, jnp.float32),
        scratch_shapes=[
            pltpu.VMEM((CHUNK + 2 * HALO, d), jnp.bfloat16),
            pltpu.VMEM((CHUNK + 2 * HALO, CONV_WIDTH), jnp.float32),
            pltpu.VMEM((CHUNK, ATTN_WIDTH), jnp.bfloat16),
            pltpu.VMEM((CHUNK, ATTN_WIDTH), jnp.bfloat16),
            pltpu.VMEM((CHUNK, CONV_WIDTH), jnp.bfloat16),
            pltpu.VMEM((CHUNK, ATTN_WIDTH), jnp.bfloat16),
            pltpu.VMEM((PAIRS, CHUNK + 2 * KV_HALO, LANES), jnp.bfloat16),
            pltpu.VMEM((PAIRS, CHUNK + 2 * KV_HALO, LANES), jnp.bfloat16),
            pltpu.VMEM((GROUP_TASKS, 2 * GRID_W, BAND), jnp.bfloat16),
            pltpu.VMEM((GROUP_TASKS, 2 * GRID_W, BAND), jnp.bfloat16),
        ],
        compiler_params=pltpu.CompilerParams(
            dimension_semantics=("arbitrary", "arbitrary"), vmem_limit_bytes=VMEM_LIMIT),
        name="hybrid_layer",
    )(x, x, x, x, w_uc, w_seg, conv_w, g_conv, bias, g_attn, w_top, w_bot, ln_g, ln_b)


def _band_bias(rpb):
    cols = np.arange(GRID_W)
    col_start = np.clip(cols - WIN_COLS // 2, 0, GRID_W - WIN_COLS)
    kc = np.arange(GRID_W)
    valid = (kc[None, :] >= col_start[:, None]) & (kc[None, :] < col_start[:, None] + WIN_COLS)
    dc_idx = kc[None, :] - cols[:, None] + (WIN_COLS - 1)
    onehot = (dc_idx[None] == np.arange(2 * WIN_COLS - 1)[:, None, None]) & valid[None]
    n_off = 2 * WIN_COLS - 1
    onehot2 = np.zeros((2, n_off, GRID_W, 2, GRID_W), np.float32)
    onehot2[0, :, :, 0, :] = onehot
    onehot2[1, :, :, 1, :] = onehot
    onehot2 = onehot2.reshape(2 * n_off, GRID_W, 2 * GRID_W)
    mask = np.tile(np.where(valid, 0.0, -np.inf).astype(np.float32), (1, 2))
    by_variant = jnp.stack([rpb[:, WIN_ROWS - 1 - v:2 * WIN_ROWS - 1 - v, :] for v in range(WIN_ROWS)])
    by_variant = by_variant.astype(jnp.float32).reshape(WIN_ROWS, N_HEADS, WIN_ROWS // 2, 2 * n_off)
    t = jnp.einsum("vhjn,ncl->vhjcl", by_variant, onehot2, precision=lax.Precision.HIGHEST)
    t = t * LOG2E + mask
    return t.reshape(WIN_ROWS, PAIRS, 2, WIN_ROWS // 2, GRID_W, 2 * GRID_W)


def kernel(x_prompt, x_sample, w_in, conv_w, rpb, g_conv, g_attn, w_out, ln_g, ln_b):
    cw = CONV_WIDTH
    y_prompt, y_sample = x_prompt, x_sample
    for d in range(DEPTH):
        w = w_in[d].astype(jnp.bfloat16)
        w_uc = jnp.concatenate([w[:, 0:cw], w[:, 2 * cw:3 * cw]], axis=1)
        w_seg = jnp.stack([w[:, c * cw:(c + 1) * cw] for c in (1, 3, 4, 5, 6, 7)])
        wo = w_out[d].astype(jnp.bfloat16)
        args = (w_uc, w_seg, conv_w[d], g_conv[d].reshape(1, -1), _band_bias(rpb[d]),
                g_attn[d].reshape(1, -1), wo[:cw], wo[cw:], ln_g[d].reshape(1, -1), ln_b[d].reshape(1, -1))
        y_prompt = _layer_call(y_prompt, *args)
        y_sample = _layer_call(y_sample, *args)
    return (y_prompt, y_sample)
```

```python
import functools

import numpy as np
import jax
import jax.numpy as jnp
from jax import lax
from jax.experimental import pallas as pl
from jax.experimental.pallas import tpu as pltpu

D_MODEL = 1024
GRID_W = 64
CONV_WIDTH = 512
CONV_GROUPS = 8
N_HEADS = 8
HEAD_DIM = 64
ATTN_WIDTH = N_HEADS * HEAD_DIM
WIN_ROWS = 8
WIN_COLS = 16
NORM_EPS = 1e-5
DEPTH = 1
DN_ALPHA = (2.0 * DEPTH) ** 0.25
LOG2E = 1.4426950408889634

LANES = 128
GROUP_W = 64
PAIRS = ATTN_WIDTH // LANES
BAND = WIN_ROWS * GRID_W
HALO = 16

CHUNK_ROWS = 16
CHUNK = CHUNK_ROWS * GRID_W
KV_HALO_ROWS = WIN_ROWS // 2
KV_HALO = KV_HALO_ROWS * GRID_W
PROJ_TM = 512
FILL_TM = 256
GROUP_ROWS = 2
GROUP_TASKS = GROUP_ROWS * PAIRS
VMEM_LIMIT = 58 * 1024 * 1024


def _group_mean_square(y):
    rows, width = y.shape
    low = lax.broadcasted_iota(jnp.int32, (rows, LANES), 1) < GROUP_W
    out = []
    for c in range(width // LANES):
        sq = y[:, c * LANES:(c + 1) * LANES]
        sq = sq * sq
        s_lo = jnp.sum(jnp.where(low, sq, 0.0), axis=-1, keepdims=True)
        s_hi = jnp.sum(jnp.where(low, 0.0, sq), axis=-1, keepdims=True)
        out.append(jnp.where(low, s_lo, s_hi))
    return jnp.concatenate(out, axis=-1) * (1.0 / GROUP_W)


def _silu(z):
    return z * jax.nn.sigmoid(z)


def _layer_kernel(xm_ref, xcp_ref, xnr_ref, wuc_ref, wseg_ref, convw_ref, gconv_ref,
                  bias_ref, gattn_ref, wtop_ref, wbot_ref, lng_ref, lnb_ref,
                  o_ref,
                  xh_s, u_s, q_s, ga_s, yc_s, ya_s, kbuf, vbuf, pbuf0, pbuf1, *, rows):
    i = pl.program_id(1)
    n = pl.num_programs(1)
    cw = CONV_WIDTH
    r0 = i * CHUNK_ROWS

    def store_pairs(buf, start, val):
        val = val.astype(jnp.bfloat16)
        for p in range(PAIRS):
            buf[p, pl.ds(start, val.shape[0]), :] = val[:, p * LANES:(p + 1) * LANES]

    xh_s[0:HALO, :] = jnp.where(i > 0, xcp_ref[0], 0.0).astype(jnp.bfloat16)
    xh_s[HALO:HALO + CHUNK, :] = xm_ref[0].astype(jnp.bfloat16)
    xh_s[HALO + CHUNK:, :] = jnp.where(i < n - 1, xnr_ref[0], 0.0).astype(jnp.bfloat16)

    @pl.when(i > 0)
    def _():
        kbuf[:, 0:2 * KV_HALO, :] = kbuf[:, CHUNK:CHUNK + 2 * KV_HALO, :]
        vbuf[:, 0:2 * KV_HALO, :] = vbuf[:, CHUNK:CHUNK + 2 * KV_HALO, :]

    @pl.when(i == 0)
    def _():
        x0 = xh_s[pl.ds(HALO, KV_HALO), :]
        store_pairs(kbuf, KV_HALO, jnp.dot(x0, wseg_ref[3], preferred_element_type=jnp.float32))
        store_pairs(vbuf, KV_HALO, jnp.dot(x0, wseg_ref[4], preferred_element_type=jnp.float32))

    def u_piece(s):
        half = (CHUNK + 2 * HALO) // 2
        xc_cc = jnp.dot(xh_s[pl.ds(s * half, half), :], wuc_ref[...], preferred_element_type=jnp.float32)
        u_s[pl.ds(s * half, half), :] = xc_cc[:, :cw] * xc_cc[:, cw:]

    def seg(start, c):
        xm = xh_s[pl.ds(HALO + start, PROJ_TM), :]
        return jnp.dot(xm, wseg_ref[c], preferred_element_type=jnp.float32)

    def q_piece(s):
        q = seg(s * PROJ_TM, 2) * (HEAD_DIM ** -0.5 * LOG2E)
        q_s[pl.ds(s * PROJ_TM, PROJ_TM), :] = q.astype(jnp.bfloat16)

    def k_piece(s):
        store_pairs(kbuf, 2 * KV_HALO + s * PROJ_TM, seg(KV_HALO + s * PROJ_TM, 3))

    def v_piece(s):
        store_pairs(vbuf, 2 * KV_HALO + s * PROJ_TM, seg(KV_HALO + s * PROJ_TM, 4))

    def gate_piece(s):
        ga_s[pl.ds(s * PROJ_TM, PROJ_TM), :] = _silu(seg(s * PROJ_TM, 5)).astype(jnp.bfloat16)

    def conv_piece(c):
        t0 = c * FILL_TM
        xm = xh_s[pl.ds(HALO + t0, FILL_TM), :]
        ext = FILL_TM + 2 * HALO
        u_ext = u_s[pl.ds(t0, ext), :]
        u_prev = pltpu.roll(u_ext, 1, 0)[HALO:HALO + FILL_TM]
        u_mid = u_ext[HALO:HALO + FILL_TM]
        u_next = pltpu.roll(u_ext, ext - 1, 0)[HALO:HALO + FILL_TM]
        conv = u_prev * convw_ref[0:1, :] + u_mid * convw_ref[1:2, :] + u_next * convw_ref[2:3, :]
        y = jnp.dot(xm, wseg_ref[0], preferred_element_type=jnp.float32) * conv
        y = y * lax.rsqrt(_group_mean_square(y) + NORM_EPS) * gconv_ref[...]
        z = jnp.dot(xm, wseg_ref[1], preferred_element_type=jnp.float32)
        yc_s[pl.ds(t0, FILL_TM), :] = (y * _silu(z)).astype(jnp.bfloat16)

    def out_piece(c):
        tok = pl.ds(c * FILL_TM, FILL_TM)
        out = jnp.dot(yc_s[tok, :], wtop_ref[...], preferred_element_type=jnp.float32)
        out = out + jnp.dot(ya_s[tok, :], wbot_ref[...], preferred_element_type=jnp.float32)
        h = DN_ALPHA * xm_ref[0, tok, :] + out
        mu = jnp.mean(h, axis=-1, keepdims=True)
        hc = h - mu
        var = jnp.mean(hc * hc, axis=-1, keepdims=True)
        o_ref[0, tok, :] = hc * lax.rsqrt(var + NORM_EPS) * lng_ref[...] + lnb_ref[...]

    low = lax.broadcasted_iota(jnp.int32, (GRID_W, LANES), 1) < GROUP_W
    ones = jnp.ones((BAND, LANES), jnp.bfloat16)

    def task(g, t):
        rl = g * GROUP_ROWS + t // PAIRS
        r = r0 + rl
        rs = jnp.clip(r - WIN_ROWS // 2, 0, rows - WIN_ROWS)
        off = pl.multiple_of((rs - r0 + KV_HALO_ROWS) * GRID_W, GRID_W)
        qoff = pl.multiple_of(rl * GRID_W, GRID_W)
        p = t % PAIRS
        return off, r - rs, qoff, p, slice(p * LANES, (p + 1) * LANES)

    def scores(g, t, pbuf):
        off, variant, qoff, p, lanes = task(g, t)
        qp = q_s[pl.ds(qoff, GRID_W), lanes]
        zero = jnp.zeros_like(qp)
        qm = jnp.concatenate([jnp.where(low, qp, zero), jnp.where(low, zero, qp)], axis=0)
        kp = kbuf[p, pl.ds(off, BAND), :]
        s = lax.dot_general(qm, kp, (((1,), (1,)), ((), ())),
                            preferred_element_type=jnp.float32)
        bias = jnp.concatenate(
            [jnp.concatenate([bias_ref[variant, p, a, j2] for j2 in range(WIN_ROWS // 2)], axis=1)
             for a in range(2)], axis=0)
        s = s + bias
        m = jnp.max(s, axis=-1, keepdims=True)
        pbuf[t] = jnp.exp2(s - m).astype(jnp.bfloat16)

    def outputs(g, t, pbuf):
        off, _, qoff, p, lanes = task(g, t)
        vp = jnp.concatenate([vbuf[p, pl.ds(off, BAND), :], ones], axis=1)
        o2 = jnp.dot(pbuf[t], vp, preferred_element_type=jnp.float32)
        o2 = o2[:, :LANES] / o2[:, LANES:]
        o = jnp.where(low, o2[:GRID_W], o2[GRID_W:])
        o = o * lax.rsqrt(_group_mean_square(o) + NORM_EPS) * gattn_ref[:, lanes]
        gate = ga_s[pl.ds(qoff, GRID_W), lanes].astype(jnp.float32)
        ya_s[pl.ds(qoff, GRID_W), lanes] = (o * gate).astype(jnp.bfloat16)

    n_groups = CHUNK_ROWS // GROUP_ROWS
    pbufs = (pbuf0, pbuf1)

    def group_step(g):
        for t in range(GROUP_TASKS):
            scores(g, t, pbufs[g % 2])
            if g > 0:
                outputs(g - 1, t, pbufs[(g - 1) % 2])

    program = (
        (u_piece, 0), (u_piece, 1),
        (q_piece, 0), (k_piece, 0), (v_piece, 0), (gate_piece, 0),
        (q_piece, 1), (k_piece, 1), (v_piece, 1), (gate_piece, 1),
        (group_step, 0),
        (group_step, 1), (conv_piece, 0),
        (group_step, 2), (conv_piece, 1),
        (group_step, 3), (out_piece, 0),
        (group_step, 4), (conv_piece, 2),
        (group_step, 5), (out_piece, 1),
        (group_step, 6), (conv_piece, 3),
        (group_step, 7), (out_piece, 2),
    )
    for piece, c in program:
        piece(c)
    for t in range(GROUP_TASKS):
        outputs(n_groups - 1, t, pbufs[(n_groups - 1) % 2])
    out_piece(3)


def _layer_call(x, w_uc, w_seg, conv_w, g_conv, bias, g_attn, w_top, w_bot, ln_g, ln_b):
    bsz, seqlen, d = x.shape
    rows = seqlen // GRID_W
    assert rows % CHUNK_ROWS == 0 and CHUNK // FILL_TM == 4 and CHUNK_ROWS // GROUP_ROWS == 8
    nc = rows // CHUNK_ROWS
    n_kv_halo = seqlen // KV_HALO

    def const(a):
        return pl.BlockSpec(a.shape, lambda b, i: (0,) * a.ndim, pipeline_mode=pl.Buffered(1))

    chunk = pl.BlockSpec((1, CHUNK, d), lambda b, i: (b, i, 0))
    return pl.pallas_call(
        functools.partial(_layer_kernel, rows=rows),
        grid=(bsz, nc),
        in_specs=[
            chunk,
            pl.BlockSpec((1, HALO, d), lambda b, i: (b, jnp.maximum(i * (CHUNK // HALO) - 1, 0), 0)),
            pl.BlockSpec((1, KV_HALO, d),
                         lambda b, i: (b, jnp.minimum((i + 1) * (CHUNK // KV_HALO), n_kv_halo - 1), 0)),
            const(w_uc), const(w_seg), const(conv_w), const(g_conv),
            const(bias), const(g_attn), const(w_top), const(w_bot), const(ln_g), const(ln_b),
        ],
        out_specs=chunk,
        out_shape=jax.ShapeDtypeStruct(x.shape, jnp.float32),
        scratch_shapes=[
            pltpu.VMEM((HALO + CHUNK + KV_HALO, d), jnp.bfloat16),
            pltpu.VMEM((CHUNK + 2 * HALO, CONV_WIDTH), jnp.float32),
            pltpu.VMEM((CHUNK, ATTN_WIDTH), jnp.bfloat16),
            pltpu.VMEM((CHUNK, ATTN_WIDTH), jnp.bfloat16),
            pltpu.VMEM((CHUNK, CONV_WIDTH), jnp.bfloat16),
            pltpu.VMEM((CHUNK, ATTN_WIDTH), jnp.bfloat16),
            pltpu.VMEM((PAIRS, CHUNK + 2 * KV_HALO, LANES), jnp.bfloat16),
            pltpu.VMEM((PAIRS, CHUNK + 2 * KV_HALO, LANES), jnp.bfloat16),
            pltpu.VMEM((GROUP_TASKS, 2 * GRID_W, BAND), jnp.bfloat16),
            pltpu.VMEM((GROUP_TASKS, 2 * GRID_W, BAND), jnp.bfloat16),
        ],
        compiler_params=pltpu.CompilerParams(
            dimension_semantics=("arbitrary", "arbitrary"), vmem_limit_bytes=VMEM_LIMIT),
        name="hybrid_layer",
    )(x, x, x, w_uc, w_seg, conv_w, g_conv, bias, g_attn, w_top, w_bot, ln_g, ln_b)


def _band_bias(rpb):
    cols = np.arange(GRID_W)
    col_start = np.clip(cols - WIN_COLS // 2, 0, GRID_W - WIN_COLS)
    kc = np.arange(GRID_W)
    valid = (kc[None, :] >= col_start[:, None]) & (kc[None, :] < col_start[:, None] + WIN_COLS)
    dc_idx = kc[None, :] - cols[:, None] + (WIN_COLS - 1)
    onehot = (dc_idx[None] == np.arange(2 * WIN_COLS - 1)[:, None, None]) & valid[None]
    n_off = 2 * WIN_COLS - 1
    onehot2 = np.zeros((2, n_off, GRID_W, 2, GRID_W), np.float32)
    onehot2[0, :, :, 0, :] = onehot
    onehot2[1, :, :, 1, :] = onehot
    onehot2 = onehot2.reshape(2 * n_off, GRID_W, 2 * GRID_W)
    mask = np.tile(np.where(valid, 0.0, -np.inf).astype(np.float32), (1, 2))
    by_variant = jnp.stack([rpb[:, WIN_ROWS - 1 - v:2 * WIN_ROWS - 1 - v, :] for v in range(WIN_ROWS)])
    by_variant = by_variant.astype(jnp.float32).reshape(WIN_ROWS, N_HEADS, WIN_ROWS // 2, 2 * n_off)
    t = jnp.einsum("vhjn,ncl->vhjcl", by_variant, onehot2, precision=lax.Precision.HIGHEST)
    t = t * LOG2E + mask
    return t.reshape(WIN_ROWS, PAIRS, 2, WIN_ROWS // 2, GRID_W, 2 * GRID_W)


def kernel(x_prompt, x_sample, w_in, conv_w, rpb, g_conv, g_attn, w_out, ln_g, ln_b):
    cw = CONV_WIDTH
    y_prompt, y_sample = x_prompt, x_sample
    for d in range(DEPTH):
        w = w_in[d].astype(jnp.bfloat16)
        w_uc = jnp.concatenate([w[:, 0:cw], w[:, 2 * cw:3 * cw]], axis=1)
        w_seg = jnp.stack([w[:, c * cw:(c + 1) * cw] for c in (1, 3, 4, 5, 6, 7)])
        wo = w_out[d].astype(jnp.bfloat16)
        args = (w_uc, w_seg, conv_w[d], g_conv[d].reshape(1, -1), _band_bias(rpb[d]),
                g_attn[d].reshape(1, -1), wo[:cw], wo[cw:], ln_g[d].reshape(1, -1), ln_b[d].reshape(1, -1))
        y_prompt = _layer_call(y_prompt, *args)
        y_sample = _layer_call(y_sample, *args)
    return (y_prompt, y_sample)
```

```python
import functools

import numpy as np
import jax
import jax.numpy as jnp
from jax import lax
from jax.experimental import pallas as pl
from jax.experimental.pallas import tpu as pltpu

D_MODEL = 1024
GRID_W = 64
CONV_WIDTH = 512
CONV_GROUPS = 8
N_HEADS = 8
HEAD_DIM = 64
ATTN_WIDTH = N_HEADS * HEAD_DIM
WIN_ROWS = 8
WIN_COLS = 16
NORM_EPS = 1e-5
DEPTH = 1
DN_ALPHA = (2.0 * DEPTH) ** 0.25
LOG2E = 1.4426950408889634

LANES = 128
GROUP_W = 64
PAIRS = ATTN_WIDTH // LANES
BAND = WIN_ROWS * GRID_W
HALO = 16

CHUNK_ROWS = 16
CHUNK = CHUNK_ROWS * GRID_W
KV_HALO_ROWS = WIN_ROWS // 2
KV_HALO = KV_HALO_ROWS * GRID_W
PROJ_TM = 512
FILL_TM = 256
GROUP_ROWS = 2
GROUP_TASKS = GROUP_ROWS * PAIRS
VMEM_LIMIT = 58 * 1024 * 1024


def _group_mean_square(y):
    rows, width = y.shape
    low = lax.broadcasted_iota(jnp.int32, (rows, LANES), 1) < GROUP_W
    out = []
    for c in range(width // LANES):
        sq = y[:, c * LANES:(c + 1) * LANES]
        sq = sq * sq
        s_lo = jnp.sum(jnp.where(low, sq, 0.0), axis=-1, keepdims=True)
        s_hi = jnp.sum(jnp.where(low, 0.0, sq), axis=-1, keepdims=True)
        out.append(jnp.where(low, s_lo, s_hi))
    return jnp.concatenate(out, axis=-1) * (1.0 / GROUP_W)


def _silu(z):
    return z * jax.nn.sigmoid(z)


def _layer_kernel(xm_ref, xcp_ref, xnr_ref, wuc_ref, wseg_ref, convw_ref, gconv_ref,
                  bias_ref, gattn_ref, wtop_ref, wbot_ref, lng_ref, lnb_ref,
                  o_ref,
                  xh_s, u_s, q_s, ga_s, yc_s, ya_s, kbuf, vbuf, pbuf0, pbuf1, *, rows):
    i = pl.program_id(1)
    n = pl.num_programs(1)
    cw = CONV_WIDTH
    r0 = i * CHUNK_ROWS

    def store_pairs(buf, start, val):
        val = val.astype(jnp.bfloat16)
        for p in range(PAIRS):
            buf[p, pl.ds(start, val.shape[0]), :] = val[:, p * LANES:(p + 1) * LANES]

    xh_s[0:HALO, :] = jnp.where(i > 0, xcp_ref[0], 0.0).astype(jnp.bfloat16)
    xh_s[HALO:HALO + CHUNK, :] = xm_ref[0].astype(jnp.bfloat16)
    xh_s[HALO + CHUNK:, :] = jnp.where(i < n - 1, xnr_ref[0], 0.0).astype(jnp.bfloat16)

    @pl.when(i > 0)
    def _():
        kbuf[:, 0:2 * KV_HALO, :] = kbuf[:, CHUNK:CHUNK + 2 * KV_HALO, :]
        vbuf[:, 0:2 * KV_HALO, :] = vbuf[:, CHUNK:CHUNK + 2 * KV_HALO, :]

    @pl.when(i == 0)
    def _():
        x0 = xh_s[pl.ds(HALO, KV_HALO), :]
        store_pairs(kbuf, KV_HALO, jnp.dot(x0, wseg_ref[3], preferred_element_type=jnp.float32))
        store_pairs(vbuf, KV_HALO, jnp.dot(x0, wseg_ref[4], preferred_element_type=jnp.float32))

    def u_piece(s):
        half = (CHUNK + 2 * HALO) // 2
        xc_cc = jnp.dot(xh_s[pl.ds(s * half, half), :], wuc_ref[...], preferred_element_type=jnp.float32)
        u_s[pl.ds(s * half, half), :] = xc_cc[:, :cw] * xc_cc[:, cw:]

    def seg(start, c):
        xm = xh_s[pl.ds(HALO + start, PROJ_TM), :]
        return jnp.dot(xm, wseg_ref[c], preferred_element_type=jnp.float32)

    def q_piece(s):
        q = seg(s * PROJ_TM, 2) * (HEAD_DIM ** -0.5 * LOG2E)
        q_s[pl.ds(s * PROJ_TM, PROJ_TM), :] = q.astype(jnp.bfloat16)

    def k_piece(s):
        store_pairs(kbuf, 2 * KV_HALO + s * PROJ_TM, seg(KV_HALO + s * PROJ_TM, 3))

    def v_piece(s):
        store_pairs(vbuf, 2 * KV_HALO + s * PROJ_TM, seg(KV_HALO + s * PROJ_TM, 4))

    def gate_piece(s):
        ga_s[pl.ds(s * PROJ_TM, PROJ_TM), :] = _silu(seg(s * PROJ_TM, 5)).astype(jnp.bfloat16)

    def conv_piece(c):
        t0 = c * FILL_TM
        xm = xh_s[pl.ds(HALO + t0, FILL_TM), :]
        ext = FILL_TM + 2 * HALO
        u_ext = u_s[pl.ds(t0, ext), :]
        u_prev = pltpu.roll(u_ext, 1, 0)[HALO:HALO + FILL_TM]
        u_mid = u_ext[HALO:HALO + FILL_TM]
        u_next = pltpu.roll(u_ext, ext - 1, 0)[HALO:HALO + FILL_TM]
        conv = u_prev * convw_ref[0:1, :] + u_mid * convw_ref[1:2, :] + u_next * convw_ref[2:3, :]
        y = jnp.dot(xm, wseg_ref[0], preferred_element_type=jnp.float32) * conv
        y = y * lax.rsqrt(_group_mean_square(y) + NORM_EPS) * gconv_ref[...]
        z = jnp.dot(xm, wseg_ref[1], preferred_element_type=jnp.float32)
        yc_s[pl.ds(t0, FILL_TM), :] = (y * _silu(z)).astype(jnp.bfloat16)

    def out_piece(c):
        tok = pl.ds(c * FILL_TM, FILL_TM)
        out = jnp.dot(yc_s[tok, :], wtop_ref[...], preferred_element_type=jnp.float32)
        out = out + jnp.dot(ya_s[tok, :], wbot_ref[...], preferred_element_type=jnp.float32)
        h = xm_ref[0, tok, :] + out
        mu = jnp.mean(h, axis=-1, keepdims=True)
        hc = h - mu
        var = jnp.mean(hc * hc, axis=-1, keepdims=True)
        o_ref[0, tok, :] = hc * lax.rsqrt(var + NORM_EPS / DN_ALPHA ** 2) * lng_ref[...] + lnb_ref[...]

    low = lax.broadcasted_iota(jnp.int32, (GRID_W, LANES), 1) < GROUP_W
    ones = jnp.ones((BAND, LANES), jnp.bfloat16)

    def task(g, t):
        rl = g * GROUP_ROWS + t // PAIRS
        r = r0 + rl
        rs = jnp.clip(r - WIN_ROWS // 2, 0, rows - WIN_ROWS)
        off = pl.multiple_of((rs - r0 + KV_HALO_ROWS) * GRID_W, GRID_W)
        qoff = pl.multiple_of(rl * GRID_W, GRID_W)
        p = t % PAIRS
        return off, r - rs, qoff, p, slice(p * LANES, (p + 1) * LANES)

    def scores(g, t, pbuf):
        off, variant, qoff, p, lanes = task(g, t)
        qp = q_s[pl.ds(qoff, GRID_W), lanes]
        zero = jnp.zeros_like(qp)
        qm = jnp.concatenate([jnp.where(low, qp, zero), jnp.where(low, zero, qp)], axis=0)
        kp = kbuf[p, pl.ds(off, BAND), :]
        s = lax.dot_general(qm, kp, (((1,), (1,)), ((), ())),
                            preferred_element_type=jnp.float32)
        bias = jnp.concatenate(
            [jnp.concatenate([bias_ref[p, a, 2 * j2 + (WIN_ROWS - 1) - variant]
                              for j2 in range(WIN_ROWS // 2)], axis=1)
             for a in range(2)], axis=0)
        s = s + bias
        m = jnp.max(s, axis=-1, keepdims=True)
        pbuf[t] = jnp.exp2(s - m).astype(jnp.bfloat16)

    def outputs(g, t, pbuf):
        off, _, qoff, p, lanes = task(g, t)
        vp = jnp.concatenate([vbuf[p, pl.ds(off, BAND), :], ones], axis=1)
        o2 = jnp.dot(pbuf[t], vp, preferred_element_type=jnp.float32)
        o2 = o2[:, :LANES] / o2[:, LANES:]
        o = jnp.where(low, o2[:GRID_W], o2[GRID_W:])
        o = o * lax.rsqrt(_group_mean_square(o) + NORM_EPS) * gattn_ref[:, lanes]
        gate = ga_s[pl.ds(qoff, GRID_W), lanes].astype(jnp.float32)
        ya_s[pl.ds(qoff, GRID_W), lanes] = (o * gate).astype(jnp.bfloat16)

    n_groups = CHUNK_ROWS // GROUP_ROWS
    pbufs = (pbuf0, pbuf1)

    def group_step(g):
        for t in range(GROUP_TASKS):
            scores(g, t, pbufs[g % 2])
            if g > 0:
                outputs(g - 1, t, pbufs[(g - 1) % 2])

    program = (
        (u_piece, 0), (u_piece, 1),
        (q_piece, 0), (k_piece, 0), (v_piece, 0), (gate_piece, 0),
        (q_piece, 1), (k_piece, 1), (v_piece, 1), (gate_piece, 1),
        (group_step, 0),
        (group_step, 1), (conv_piece, 0),
        (group_step, 2), (conv_piece, 1),
        (group_step, 3), (out_piece, 0),
        (group_step, 4), (conv_piece, 2),
        (group_step, 5), (out_piece, 1),
        (group_step, 6), (conv_piece, 3),
        (group_step, 7), (out_piece, 2),
    )
    for piece, c in program:
        piece(c)
    for t in range(GROUP_TASKS):
        outputs(n_groups - 1, t, pbufs[(n_groups - 1) % 2])
    out_piece(3)


def _layer_call(x, w_uc, w_seg, conv_w, g_conv, bias, g_attn, w_top, w_bot, ln_g, ln_b):
    bsz, seqlen, d = x.shape
    rows = seqlen // GRID_W
    assert rows % CHUNK_ROWS == 0 and CHUNK // FILL_TM == 4 and CHUNK_ROWS // GROUP_ROWS == 8
    nc = rows // CHUNK_ROWS
    n_kv_halo = seqlen // KV_HALO

    def const(a):
        return pl.BlockSpec(a.shape, lambda b, i: (0,) * a.ndim, pipeline_mode=pl.Buffered(1))

    chunk = pl.BlockSpec((1, CHUNK, d), lambda b, i: (b, i, 0))
    return pl.pallas_call(
        functools.partial(_layer_kernel, rows=rows),
        grid=(bsz, nc),
        in_specs=[
            chunk,
            pl.BlockSpec((1, HALO, d), lambda b, i: (b, jnp.maximum(i * (CHUNK // HALO) - 1, 0), 0)),
            pl.BlockSpec((1, KV_HALO, d),
                         lambda b, i: (b, jnp.minimum((i + 1) * (CHUNK // KV_HALO), n_kv_halo - 1), 0)),
            const(w_uc), const(w_seg), const(conv_w), const(g_conv),
            const(bias), const(g_attn), const(w_top), const(w_bot), const(ln_g), const(ln_b),
        ],
        out_specs=chunk,
        out_shape=jax.ShapeDtypeStruct(x.shape, jnp.float32),
        scratch_shapes=[
            pltpu.VMEM((HALO + CHUNK + KV_HALO, d), jnp.bfloat16),
            pltpu.VMEM((CHUNK + 2 * HALO, CONV_WIDTH), jnp.float32),
            pltpu.VMEM((CHUNK, ATTN_WIDTH), jnp.bfloat16),
            pltpu.VMEM((CHUNK, ATTN_WIDTH), jnp.bfloat16),
            pltpu.VMEM((CHUNK, CONV_WIDTH), jnp.bfloat16),
            pltpu.VMEM((CHUNK, ATTN_WIDTH), jnp.bfloat16),
            pltpu.VMEM((PAIRS, CHUNK + 2 * KV_HALO, LANES), jnp.bfloat16),
            pltpu.VMEM((PAIRS, CHUNK + 2 * KV_HALO, LANES), jnp.bfloat16),
            pltpu.VMEM((GROUP_TASKS, 2 * GRID_W, BAND), jnp.bfloat16),
            pltpu.VMEM((GROUP_TASKS, 2 * GRID_W, BAND), jnp.bfloat16),
        ],
        compiler_params=pltpu.CompilerParams(
            dimension_semantics=("arbitrary", "arbitrary"), vmem_limit_bytes=VMEM_LIMIT),
        name="hybrid_layer",
    )(x, x, x, w_uc, w_seg, conv_w, g_conv, bias, g_attn, w_top, w_bot, ln_g, ln_b)


def _band_bias(rpb):
    cols = np.arange(GRID_W)
    col_start = np.clip(cols - WIN_COLS // 2, 0, GRID_W - WIN_COLS)
    kc = np.arange(GRID_W)
    valid = (kc[None, :] >= col_start[:, None]) & (kc[None, :] < col_start[:, None] + WIN_COLS)
    dc_idx = kc[None, :] - cols[:, None] + (WIN_COLS - 1)
    onehot = (dc_idx[None] == np.arange(2 * WIN_COLS - 1)[:, None, None]) & valid[None]
    n_off = 2 * WIN_COLS - 1
    onehot2 = np.zeros((2, n_off, GRID_W, 2, GRID_W), np.float32)
    onehot2[0, :, :, 0, :] = onehot
    onehot2[1, :, :, 1, :] = onehot
    onehot2 = onehot2.reshape(2 * n_off, GRID_W, 2 * GRID_W)
    mask = np.tile(np.where(valid, 0.0, -np.inf).astype(np.float32), (1, 2))
    n_d = 2 * WIN_ROWS - 2
    row_pairs = jnp.stack([rpb[:, 0:n_d, :], rpb[:, 1:n_d + 1, :]], axis=2).astype(jnp.float32)
    t = jnp.einsum("hdn,ncl->hdcl", row_pairs.reshape(N_HEADS, n_d, 2 * n_off), onehot2,
                   precision=lax.Precision.HIGHEST)
    t = t * LOG2E + mask
    return t.reshape(PAIRS, 2, n_d, GRID_W, 2 * GRID_W)


def kernel(x_prompt, x_sample, w_in, conv_w, rpb, g_conv, g_attn, w_out, ln_g, ln_b):
    cw = CONV_WIDTH
    y_prompt, y_sample = x_prompt, x_sample
    for d in range(DEPTH):
        w = w_in[d].astype(jnp.bfloat16)
        w_uc = jnp.concatenate([w[:, 0:cw], w[:, 2 * cw:3 * cw]], axis=1)
        w_seg = jnp.stack([w[:, c * cw:(c + 1) * cw] for c in (1, 3, 4, 5, 6, 7)])
        wo = (w_out[d] * (1.0 / DN_ALPHA)).astype(jnp.bfloat16)
        args = (w_uc, w_seg, conv_w[d], g_conv[d].reshape(1, -1), _band_bias(rpb[d]),
                g_attn[d].reshape(1, -1), wo[:cw], wo[cw:], ln_g[d].reshape(1, -1), ln_b[d].reshape(1, -1))
        y_prompt = _layer_call(y_prompt, *args)
        y_sample = _layer_call(y_sample, *args)
    return (y_prompt, y_sample)
```

```python
import functools

import numpy as np
import jax
import jax.numpy as jnp
from jax import lax
from jax.experimental import pallas as pl
from jax.experimental.pallas import tpu as pltpu

D_MODEL = 1024
GRID_W = 64
CONV_WIDTH = 512
CONV_GROUPS = 8
N_HEADS = 8
HEAD_DIM = 64
ATTN_WIDTH = N_HEADS * HEAD_DIM
WIN_ROWS = 8
WIN_COLS = 16
NORM_EPS = 1e-5
DEPTH = 1
DN_ALPHA = (2.0 * DEPTH) ** 0.25
LOG2E = 1.4426950408889634

LANES = 128
GROUP_W = 64
PAIRS = ATTN_WIDTH // LANES
BAND = WIN_ROWS * GRID_W
HALO = 16

CHUNK_ROWS = 16
CHUNK = CHUNK_ROWS * GRID_W
KV_HALO_ROWS = WIN_ROWS // 2
KV_HALO = KV_HALO_ROWS * GRID_W
PROJ_TM = 512
FILL_TM = 256
GROUP_ROWS = 2
GROUP_TASKS = GROUP_ROWS * PAIRS
VMEM_LIMIT = 58 * 1024 * 1024


def _group_mean_square(y):
    rows, width = y.shape
    low = lax.broadcasted_iota(jnp.int32, (rows, LANES), 1) < GROUP_W
    out = []
    for c in range(width // LANES):
        sq = y[:, c * LANES:(c + 1) * LANES]
        sq = sq * sq
        s_lo = jnp.sum(jnp.where(low, sq, 0.0), axis=-1, keepdims=True)
        s_hi = jnp.sum(jnp.where(low, 0.0, sq), axis=-1, keepdims=True)
        out.append(jnp.where(low, s_lo, s_hi))
    return jnp.concatenate(out, axis=-1) * (1.0 / GROUP_W)


def _silu(z):
    return z * jax.nn.sigmoid(z)


def _layer_kernel(xm_ref, xcp_ref, xnr_ref, wuc_ref, wseg_ref, convw_ref, gconv_ref,
                  bias_ref, gattn_ref, wtop_ref, wbot_ref, lng_ref, lnb_ref,
                  o_ref,
                  xh_s, u_s, q_s, ga_s, yc_s, ya_s, kbuf, vbuf, pbuf0, pbuf1, *, rows):
    i = pl.program_id(1)
    n = pl.num_programs(1)
    cw = CONV_WIDTH
    r0 = i * CHUNK_ROWS

    def store_pairs(buf, start, val):
        val = val.astype(jnp.bfloat16)
        for p in range(PAIRS):
            buf[p, pl.ds(start, val.shape[0]), :] = val[:, p * LANES:(p + 1) * LANES]

    xh_s[0:HALO, :] = jnp.where(i > 0, xcp_ref[0], 0.0).astype(jnp.bfloat16)
    xh_s[HALO:HALO + CHUNK, :] = xm_ref[0].astype(jnp.bfloat16)
    xh_s[HALO + CHUNK:, :] = jnp.where(i < n - 1, xnr_ref[0], 0.0).astype(jnp.bfloat16)

    @pl.when(i > 0)
    def _():
        kbuf[:, 0:2 * KV_HALO, :] = kbuf[:, CHUNK:CHUNK + 2 * KV_HALO, :]
        vbuf[:, 0:2 * KV_HALO, :] = vbuf[:, CHUNK:CHUNK + 2 * KV_HALO, :]

    @pl.when(i == 0)
    def _():
        x0 = xh_s[pl.ds(HALO, KV_HALO), :]
        store_pairs(kbuf, KV_HALO, jnp.dot(x0, wseg_ref[3], preferred_element_type=jnp.float32))
        store_pairs(vbuf, KV_HALO, jnp.dot(x0, wseg_ref[4], preferred_element_type=jnp.float32))

    def u_piece(s):
        half = (CHUNK + 2 * HALO) // 2
        xc_cc = jnp.dot(xh_s[pl.ds(s * half, half), :], wuc_ref[...], preferred_element_type=jnp.float32)
        u_s[pl.ds(s * half, half), :] = xc_cc[:, :cw] * xc_cc[:, cw:]

    def seg(start, c):
        xm = xh_s[pl.ds(HALO + start, PROJ_TM), :]
        return jnp.dot(xm, wseg_ref[c], preferred_element_type=jnp.float32)

    def q_piece(s):
        q = seg(s * PROJ_TM, 2) * (HEAD_DIM ** -0.5 * LOG2E)
        q_s[pl.ds(s * PROJ_TM, PROJ_TM), :] = q.astype(jnp.bfloat16)

    def k_piece(s):
        store_pairs(kbuf, 2 * KV_HALO + s * PROJ_TM, seg(KV_HALO + s * PROJ_TM, 3))

    def v_piece(s):
        store_pairs(vbuf, 2 * KV_HALO + s * PROJ_TM, seg(KV_HALO + s * PROJ_TM, 4))

    def gate_piece(s):
        ga_s[pl.ds(s * PROJ_TM, PROJ_TM), :] = _silu(seg(s * PROJ_TM, 5)).astype(jnp.bfloat16)

    def conv_piece(c):
        t0 = c * FILL_TM
        xm = xh_s[pl.ds(HALO + t0, FILL_TM), :]
        ext = FILL_TM + 2 * HALO
        u_ext = u_s[pl.ds(t0, ext), :]
        u_prev = pltpu.roll(u_ext, 1, 0)[HALO:HALO + FILL_TM]
        u_mid = u_ext[HALO:HALO + FILL_TM]
        u_next = pltpu.roll(u_ext, ext - 1, 0)[HALO:HALO + FILL_TM]
        conv = u_prev * convw_ref[0:1, :] + u_mid * convw_ref[1:2, :] + u_next * convw_ref[2:3, :]
        y = jnp.dot(xm, wseg_ref[0], preferred_element_type=jnp.float32) * conv
        y = y * lax.rsqrt(_group_mean_square(y) + NORM_EPS) * gconv_ref[...]
        z = jnp.dot(xm, wseg_ref[1], preferred_element_type=jnp.float32)
        yc_s[pl.ds(t0, FILL_TM), :] = (y * _silu(z)).astype(jnp.bfloat16)

    def out_piece(c):
        tok = pl.ds(c * FILL_TM, FILL_TM)
        out = jnp.dot(yc_s[tok, :], wtop_ref[...], preferred_element_type=jnp.float32)
        out = out + jnp.dot(ya_s[tok, :], wbot_ref[...], preferred_element_type=jnp.float32)
        h = xm_ref[0, tok, :] + out
        mu = jnp.mean(h, axis=-1, keepdims=True)
        hc = h - mu
        var = jnp.mean(hc * hc, axis=-1, keepdims=True)
        o_ref[0, tok, :] = hc * lax.rsqrt(var + NORM_EPS / DN_ALPHA ** 2) * lng_ref[...] + lnb_ref[...]

    low = lax.broadcasted_iota(jnp.int32, (GRID_W, LANES), 1) < GROUP_W
    ones = jnp.ones((BAND, LANES), jnp.bfloat16)

    def task(g, t):
        rl = g * GROUP_ROWS + t // PAIRS
        r = r0 + rl
        rs = jnp.clip(r - WIN_ROWS // 2, 0, rows - WIN_ROWS)
        off = pl.multiple_of((rs - r0 + KV_HALO_ROWS) * GRID_W, GRID_W)
        qoff = pl.multiple_of(rl * GRID_W, GRID_W)
        p = t % PAIRS
        return off, r - rs, qoff, p, slice(p * LANES, (p + 1) * LANES)

    def scores(g, t, pbuf):
        off, variant, qoff, p, lanes = task(g, t)
        qp = q_s[pl.ds(qoff, GRID_W), lanes]
        zero = jnp.zeros_like(qp)
        qm = jnp.concatenate([jnp.where(low, qp, zero), jnp.where(low, zero, qp)], axis=0)
        kp = kbuf[p, pl.ds(off, BAND), :]
        s = lax.dot_general(qm, kp, (((1,), (1,)), ((), ())),
                            preferred_element_type=jnp.float32)
        bias = jnp.concatenate(
            [jnp.concatenate([bias_ref[p, a, 2 * j2 + (WIN_ROWS - 1) - variant]
                              for j2 in range(WIN_ROWS // 2)], axis=1)
             for a in range(2)], axis=0)
        s = (s + bias).astype(jnp.bfloat16)
        m = jnp.max(s, axis=-1, keepdims=True)
        pbuf[t] = jnp.exp2(s - m)

    def outputs(g, t, pbuf):
        off, _, qoff, p, lanes = task(g, t)
        vp = jnp.concatenate([vbuf[p, pl.ds(off, BAND), :], ones], axis=1)
        o2 = jnp.dot(pbuf[t], vp, preferred_element_type=jnp.float32)
        o2 = o2[:, :LANES] / o2[:, LANES:]
        o = jnp.where(low, o2[:GRID_W], o2[GRID_W:])
        o = o * lax.rsqrt(_group_mean_square(o) + NORM_EPS) * gattn_ref[:, lanes]
        gate = ga_s[pl.ds(qoff, GRID_W), lanes].astype(jnp.float32)
        ya_s[pl.ds(qoff, GRID_W), lanes] = (o * gate).astype(jnp.bfloat16)

    n_groups = CHUNK_ROWS // GROUP_ROWS
    pbufs = (pbuf0, pbuf1)

    def group_step(g):
        for t in range(GROUP_TASKS):
            scores(g, t, pbufs[g % 2])
            if g > 0:
                outputs(g - 1, t, pbufs[(g - 1) % 2])

    program = (
        (u_piece, 0), (u_piece, 1),
        (q_piece, 0), (k_piece, 0), (v_piece, 0), (gate_piece, 0),
        (q_piece, 1), (k_piece, 1), (v_piece, 1), (gate_piece, 1),
        (group_step, 0),
        (group_step, 1), (conv_piece, 0),
        (group_step, 2), (conv_piece, 1),
        (group_step, 3), (out_piece, 0),
        (group_step, 4), (conv_piece, 2),
        (group_step, 5), (out_piece, 1),
        (group_step, 6), (conv_piece, 3),
        (group_step, 7), (out_piece, 2),
    )
    for piece, c in program:
        piece(c)
    for t in range(GROUP_TASKS):
        outputs(n_groups - 1, t, pbufs[(n_groups - 1) % 2])
    out_piece(3)


def _layer_call(x, w_uc, w_seg, conv_w, g_conv, bias, g_attn, w_top, w_bot, ln_g, ln_b):
    bsz, seqlen, d = x.shape
    rows = seqlen // GRID_W
    assert rows % CHUNK_ROWS == 0 and CHUNK // FILL_TM == 4 and CHUNK_ROWS // GROUP_ROWS == 8
    nc = rows // CHUNK_ROWS
    n_kv_halo = seqlen // KV_HALO

    def const(a):
        return pl.BlockSpec(a.shape, lambda b, i: (0,) * a.ndim, pipeline_mode=pl.Buffered(1))

    chunk = pl.BlockSpec((1, CHUNK, d), lambda b, i: (b, i, 0))
    return pl.pallas_call(
        functools.partial(_layer_kernel, rows=rows),
        grid=(bsz, nc),
        in_specs=[
            chunk,
            pl.BlockSpec((1, HALO, d), lambda b, i: (b, jnp.maximum(i * (CHUNK // HALO) - 1, 0), 0)),
            pl.BlockSpec((1, KV_HALO, d),
                         lambda b, i: (b, jnp.minimum((i + 1) * (CHUNK // KV_HALO), n_kv_halo - 1), 0)),
            const(w_uc), const(w_seg), const(conv_w), const(g_conv),
            const(bias), const(g_attn), const(w_top), const(w_bot), const(ln_g), const(ln_b),
        ],
        out_specs=chunk,
        out_shape=jax.ShapeDtypeStruct(x.shape, jnp.float32),
        scratch_shapes=[
            pltpu.VMEM((HALO + CHUNK + KV_HALO, d), jnp.bfloat16),
            pltpu.VMEM((CHUNK + 2 * HALO, CONV_WIDTH), jnp.float32),
            pltpu.VMEM((CHUNK, ATTN_WIDTH), jnp.bfloat16),
            pltpu.VMEM((CHUNK, ATTN_WIDTH), jnp.bfloat16),
            pltpu.VMEM((CHUNK, CONV_WIDTH), jnp.bfloat16),
            pltpu.VMEM((CHUNK, ATTN_WIDTH), jnp.bfloat16),
            pltpu.VMEM((PAIRS, CHUNK + 2 * KV_HALO, LANES), jnp.bfloat16),
            pltpu.VMEM((PAIRS, CHUNK + 2 * KV_HALO, LANES), jnp.bfloat16),
            pltpu.VMEM((GROUP_TASKS, 2 * GRID_W, BAND), jnp.bfloat16),
            pltpu.VMEM((GROUP_TASKS, 2 * GRID_W, BAND), jnp.bfloat16),
        ],
        compiler_params=pltpu.CompilerParams(
            dimension_semantics=("arbitrary", "arbitrary"), vmem_limit_bytes=VMEM_LIMIT),
        name="hybrid_layer",
    )(x, x, x, w_uc, w_seg, conv_w, g_conv, bias, g_attn, w_top, w_bot, ln_g, ln_b)


def _band_bias(rpb):
    cols = np.arange(GRID_W)
    col_start = np.clip(cols - WIN_COLS // 2, 0, GRID_W - WIN_COLS)
    kc = np.arange(GRID_W)
    valid = (kc[None, :] >= col_start[:, None]) & (kc[None, :] < col_start[:, None] + WIN_COLS)
    dc_idx = kc[None, :] - cols[:, None] + (WIN_COLS - 1)
    onehot = (dc_idx[None] == np.arange(2 * WIN_COLS - 1)[:, None, None]) & valid[None]
    n_off = 2 * WIN_COLS - 1
    onehot2 = np.zeros((2, n_off, GRID_W, 2, GRID_W), np.float32)
    onehot2[0, :, :, 0, :] = onehot
    onehot2[1, :, :, 1, :] = onehot
    onehot2 = onehot2.reshape(2 * n_off, GRID_W, 2 * GRID_W)
    mask = np.tile(np.where(valid, 0.0, -np.inf).astype(np.float32), (1, 2))
    n_d = 2 * WIN_ROWS - 2
    row_pairs = jnp.stack([rpb[:, 0:n_d, :], rpb[:, 1:n_d + 1, :]], axis=2).astype(jnp.float32)
    t = jnp.einsum("hdn,ncl->hdcl", row_pairs.reshape(N_HEADS, n_d, 2 * n_off), onehot2,
                   precision=lax.Precision.HIGHEST)
    t = t * LOG2E + mask
    return t.reshape(PAIRS, 2, n_d, GRID_W, 2 * GRID_W)


def kernel(x_prompt, x_sample, w_in, conv_w, rpb, g_conv, g_attn, w_out, ln_g, ln_b):
    cw = CONV_WIDTH
    y_prompt, y_sample = x_prompt, x_sample
    for d in range(DEPTH):
        w = w_in[d].astype(jnp.bfloat16)
        w_uc = jnp.concatenate([w[:, 0:cw], w[:, 2 * cw:3 * cw]], axis=1)
        w_seg = jnp.stack([w[:, c * cw:(c + 1) * cw] for c in (1, 3, 4, 5, 6, 7)])
        wo = (w_out[d] * (1.0 / DN_ALPHA)).astype(jnp.bfloat16)
        args = (w_uc, w_seg, conv_w[d], g_conv[d].reshape(1, -1), _band_bias(rpb[d]),
                g_attn[d].reshape(1, -1), wo[:cw], wo[cw:], ln_g[d].reshape(1, -1), ln_b[d].reshape(1, -1))
        y_prompt = _layer_call(y_prompt, *args)
        y_sample = _layer_call(y_sample, *args)
    return (y_prompt, y_sample)
```

```python
import functools

import numpy as np
import jax
import jax.numpy as jnp
from jax import lax
from jax.experimental import pallas as pl
from jax.experimental.pallas import tpu as pltpu

D_MODEL = 1024
GRID_W = 64
CONV_WIDTH = 512
CONV_GROUPS = 8
N_HEADS = 8
HEAD_DIM = 64
ATTN_WIDTH = N_HEADS * HEAD_DIM
WIN_ROWS = 8
WIN_COLS = 16
NORM_EPS = 1e-5
DEPTH = 1
DN_ALPHA = (2.0 * DEPTH) ** 0.25
LOG2E = 1.4426950408889634

LANES = 128
GROUP_W = 64
PAIRS = ATTN_WIDTH // LANES
BAND = WIN_ROWS * GRID_W
HALO = 16

CHUNK_ROWS = 16
CHUNK = CHUNK_ROWS * GRID_W
KV_HALO_ROWS = WIN_ROWS // 2
KV_HALO = KV_HALO_ROWS * GRID_W
PROJ_TM = 512
FILL_TM = 256
GROUP_ROWS = 2
GROUP_TASKS = GROUP_ROWS * PAIRS
VMEM_LIMIT = 58 * 1024 * 1024


def _group_mean_square(y):
    rows, width = y.shape
    low = lax.broadcasted_iota(jnp.int32, (rows, LANES), 1) < GROUP_W
    out = []
    for c in range(width // LANES):
        sq = y[:, c * LANES:(c + 1) * LANES]
        sq = sq * sq
        s_lo = jnp.sum(jnp.where(low, sq, 0.0), axis=-1, keepdims=True)
        s_hi = jnp.sum(jnp.where(low, 0.0, sq), axis=-1, keepdims=True)
        out.append(jnp.where(low, s_lo, s_hi))
    return jnp.concatenate(out, axis=-1) * (1.0 / GROUP_W)


def _silu(z):
    return z * jax.nn.sigmoid(z)


def _layer_kernel(xm_ref, xcp_ref, xnr_ref, wuc_ref, wseg_ref, convw_ref, gconv_ref,
                  bias_ref, gattn_ref, wtop_ref, wbot_ref, lng_ref, lnb_ref,
                  o_ref,
                  xh_s, u_s, q_s, ga_s, yc_s, ya_s, kbuf, vbuf, pbuf0, pbuf1, *, rows):
    i = pl.program_id(1)
    n = pl.num_programs(1)
    cw = CONV_WIDTH
    r0 = i * CHUNK_ROWS

    def store_pairs(buf, start, val):
        val = val.astype(jnp.bfloat16)
        for p in range(PAIRS):
            buf[p, pl.ds(start, val.shape[0]), :] = val[:, p * LANES:(p + 1) * LANES]

    xh_s[0:HALO, :] = jnp.where(i > 0, xcp_ref[0], 0.0).astype(jnp.bfloat16)
    xh_s[HALO:HALO + CHUNK, :] = xm_ref[0].astype(jnp.bfloat16)
    xh_s[HALO + CHUNK:, :] = jnp.where(i < n - 1, xnr_ref[0], 0.0).astype(jnp.bfloat16)

    @pl.when(i > 0)
    def _():
        kbuf[:, 0:2 * KV_HALO, :] = kbuf[:, CHUNK:CHUNK + 2 * KV_HALO, :]
        vbuf[:, 0:2 * KV_HALO, :] = vbuf[:, CHUNK:CHUNK + 2 * KV_HALO, :]

    @pl.when(i == 0)
    def _():
        x0 = xh_s[pl.ds(HALO, KV_HALO), :]
        store_pairs(kbuf, KV_HALO, jnp.dot(x0, wseg_ref[3], preferred_element_type=jnp.float32))
        store_pairs(vbuf, KV_HALO, jnp.dot(x0, wseg_ref[4], preferred_element_type=jnp.float32))

    def u_piece(s):
        half = (CHUNK + 2 * HALO) // 2
        xc_cc = jnp.dot(xh_s[pl.ds(s * half, half), :], wuc_ref[...], preferred_element_type=jnp.float32)
        u_s[pl.ds(s * half, half), :] = xc_cc[:, :cw] * xc_cc[:, cw:]

    def seg(start, c):
        xm = xh_s[pl.ds(HALO + start, PROJ_TM), :]
        return jnp.dot(xm, wseg_ref[c], preferred_element_type=jnp.float32)

    def q_piece(s):
        q = seg(s * PROJ_TM, 2) * (HEAD_DIM ** -0.5 * LOG2E)
        q_s[pl.ds(s * PROJ_TM, PROJ_TM), :] = q.astype(jnp.bfloat16)

    def k_piece(s):
        store_pairs(kbuf, 2 * KV_HALO + s * PROJ_TM, seg(KV_HALO + s * PROJ_TM, 3))

    def v_piece(s):
        store_pairs(vbuf, 2 * KV_HALO + s * PROJ_TM, seg(KV_HALO + s * PROJ_TM, 4))

    def gate_piece(s):
        ga_s[pl.ds(s * PROJ_TM, PROJ_TM), :] = _silu(seg(s * PROJ_TM, 5)).astype(jnp.bfloat16)

    def conv_piece(c):
        t0 = c * FILL_TM
        xm = xh_s[pl.ds(HALO + t0, FILL_TM), :]
        ext = FILL_TM + 2 * HALO
        u_ext = u_s[pl.ds(t0, ext), :]
        u_prev = pltpu.roll(u_ext, 1, 0)[HALO:HALO + FILL_TM]
        u_mid = u_ext[HALO:HALO + FILL_TM]
        u_next = pltpu.roll(u_ext, ext - 1, 0)[HALO:HALO + FILL_TM]
        conv = u_prev * convw_ref[0:1, :] + u_mid * convw_ref[1:2, :] + u_next * convw_ref[2:3, :]
        y = jnp.dot(xm, wseg_ref[0], preferred_element_type=jnp.float32) * conv
        y = y * lax.rsqrt(_group_mean_square(y) + NORM_EPS) * gconv_ref[...]
        z = jnp.dot(xm, wseg_ref[1], preferred_element_type=jnp.float32)
        yc_s[pl.ds(t0, FILL_TM), :] = (y * _silu(z)).astype(jnp.bfloat16)

    def out_piece(c):
        tok = pl.ds(c * FILL_TM, FILL_TM)
        out = jnp.dot(yc_s[tok, :], wtop_ref[...], preferred_element_type=jnp.float32)
        out = out + jnp.dot(ya_s[tok, :], wbot_ref[...], preferred_element_type=jnp.float32)
        h = xm_ref[0, tok, :] + out
        mu = jnp.mean(h, axis=-1, keepdims=True)
        hc = h - mu
        var = jnp.mean(hc * hc, axis=-1, keepdims=True)
        o_ref[0, tok, :] = hc * lax.rsqrt(var + NORM_EPS / DN_ALPHA ** 2) * lng_ref[...] + lnb_ref[...]

    low = lax.broadcasted_iota(jnp.int32, (GRID_W, LANES), 1) < GROUP_W
    ones = jnp.ones((BAND, LANES), jnp.bfloat16)

    def task(g, t):
        rl = g * GROUP_ROWS + t // PAIRS
        r = r0 + rl
        rs = jnp.clip(r - WIN_ROWS // 2, 0, rows - WIN_ROWS)
        off = pl.multiple_of((rs - r0 + KV_HALO_ROWS) * GRID_W, GRID_W)
        qoff = pl.multiple_of(rl * GRID_W, GRID_W)
        p = t % PAIRS
        return off, r - rs, qoff, p, slice(p * LANES, (p + 1) * LANES)

    def scores(g, t, pbuf):
        off, variant, qoff, p, lanes = task(g, t)
        qp = q_s[pl.ds(qoff, GRID_W), lanes]
        zero = jnp.zeros_like(qp)
        qm = jnp.concatenate([jnp.where(low, qp, zero), jnp.where(low, zero, qp)], axis=0)
        kp = kbuf[p, pl.ds(off, BAND), :]
        s = lax.dot_general(qm, kp, (((1,), (1,)), ((), ())),
                            preferred_element_type=jnp.float32)
        bias = jnp.concatenate(
            [jnp.concatenate([bias_ref[p, a, 2 * j2 + (WIN_ROWS - 1) - variant]
                              for j2 in range(WIN_ROWS // 2)], axis=1)
             for a in range(2)], axis=0)
        s = (s + bias).astype(jnp.bfloat16)
        m = jnp.max(s, axis=-1, keepdims=True)
        pbuf[t] = jnp.exp2(s - m)

    def outputs(g, t, pbuf):
        off, _, qoff, p, lanes = task(g, t)
        vp = jnp.concatenate([vbuf[p, pl.ds(off, BAND), :], ones], axis=1)
        o2 = jnp.dot(pbuf[t], vp, preferred_element_type=jnp.float32)
        num = jnp.where(low, o2[:GRID_W, :LANES], o2[GRID_W:, :LANES])
        den = jnp.where(low, o2[:GRID_W, LANES:], o2[GRID_W:, LANES:])
        o = num / den
        o = o * lax.rsqrt(_group_mean_square(o) + NORM_EPS) * gattn_ref[:, lanes]
        gate = ga_s[pl.ds(qoff, GRID_W), lanes].astype(jnp.float32)
        ya_s[pl.ds(qoff, GRID_W), lanes] = (o * gate).astype(jnp.bfloat16)

    n_groups = CHUNK_ROWS // GROUP_ROWS
    pbufs = (pbuf0, pbuf1)

    def group_step(g):
        for t in range(GROUP_TASKS):
            scores(g, t, pbufs[g % 2])
            if g > 0:
                outputs(g - 1, t, pbufs[(g - 1) % 2])

    program = (
        (u_piece, 0), (u_piece, 1),
        (q_piece, 0), (k_piece, 0), (v_piece, 0), (gate_piece, 0),
        (q_piece, 1), (k_piece, 1), (v_piece, 1), (gate_piece, 1),
        (group_step, 0),
        (group_step, 1), (conv_piece, 0),
        (group_step, 2), (conv_piece, 1),
        (group_step, 3), (out_piece, 0),
        (group_step, 4), (conv_piece, 2),
        (group_step, 5), (out_piece, 1),
        (group_step, 6), (conv_piece, 3),
        (group_step, 7), (out_piece, 2),
    )
    for piece, c in program:
        piece(c)
    for t in range(GROUP_TASKS):
        outputs(n_groups - 1, t, pbufs[(n_groups - 1) % 2])
    out_piece(3)


def _layer_call(x, w_uc, w_seg, conv_w, g_conv, bias, g_attn, w_top, w_bot, ln_g, ln_b):
    bsz, seqlen, d = x.shape
    rows = seqlen // GRID_W
    assert rows % CHUNK_ROWS == 0 and CHUNK // FILL_TM == 4 and CHUNK_ROWS // GROUP_ROWS == 8
    nc = rows // CHUNK_ROWS
    n_kv_halo = seqlen // KV_HALO

    def const(a):
        return pl.BlockSpec(a.shape, lambda b, i: (0,) * a.ndim, pipeline_mode=pl.Buffered(1))

    chunk = pl.BlockSpec((1, CHUNK, d), lambda b, i: (b, i, 0))
    return pl.pallas_call(
        functools.partial(_layer_kernel, rows=rows),
        grid=(bsz, nc),
        in_specs=[
            chunk,
            pl.BlockSpec((1, HALO, d), lambda b, i: (b, jnp.maximum(i * (CHUNK // HALO) - 1, 0), 0)),
            pl.BlockSpec((1, KV_HALO, d),
                         lambda b, i: (b, jnp.minimum((i + 1) * (CHUNK // KV_HALO), n_kv_halo - 1), 0)),
            const(w_uc), const(w_seg), const(conv_w), const(g_conv),
            const(bias), const(g_attn), const(w_top), const(w_bot), const(ln_g), const(ln_b),
        ],
        out_specs=chunk,
        out_shape=jax.ShapeDtypeStruct(x.shape, jnp.float32),
        scratch_shapes=[
            pltpu.VMEM((HALO + CHUNK + KV_HALO, d), jnp.bfloat16),
            pltpu.VMEM((CHUNK + 2 * HALO, CONV_WIDTH), jnp.float32),
            pltpu.VMEM((CHUNK, ATTN_WIDTH), jnp.bfloat16),
            pltpu.VMEM((CHUNK, ATTN_WIDTH), jnp.bfloat16),
            pltpu.VMEM((CHUNK, CONV_WIDTH), jnp.bfloat16),
            pltpu.VMEM((CHUNK, ATTN_WIDTH), jnp.bfloat16),
            pltpu.VMEM((PAIRS, CHUNK + 2 * KV_HALO, LANES), jnp.bfloat16),
            pltpu.VMEM((PAIRS, CHUNK + 2 * KV_HALO, LANES), jnp.bfloat16),
            pltpu.VMEM((GROUP_TASKS, 2 * GRID_W, BAND), jnp.bfloat16),
            pltpu.VMEM((GROUP_TASKS, 2 * GRID_W, BAND), jnp.bfloat16),
        ],
        compiler_params=pltpu.CompilerParams(
            dimension_semantics=("arbitrary", "arbitrary"), vmem_limit_bytes=VMEM_LIMIT),
        name="hybrid_layer",
    )(x, x, x, w_uc, w_seg, conv_w, g_conv, bias, g_attn, w_top, w_bot, ln_g, ln_b)


def _band_bias(rpb):
    cols = np.arange(GRID_W)
    col_start = np.clip(cols - WIN_COLS // 2, 0, GRID_W - WIN_COLS)
    kc = np.arange(GRID_W)
    valid = (kc[None, :] >= col_start[:, None]) & (kc[None, :] < col_start[:, None] + WIN_COLS)
    dc_idx = kc[None, :] - cols[:, None] + (WIN_COLS - 1)
    onehot = (dc_idx[None] == np.arange(2 * WIN_COLS - 1)[:, None, None]) & valid[None]
    n_off = 2 * WIN_COLS - 1
    onehot2 = np.zeros((2, n_off, GRID_W, 2, GRID_W), np.float32)
    onehot2[0, :, :, 0, :] = onehot
    onehot2[1, :, :, 1, :] = onehot
    onehot2 = onehot2.reshape(2 * n_off, GRID_W, 2 * GRID_W)
    mask = np.tile(np.where(valid, 0.0, -np.inf).astype(np.float32), (1, 2))
    n_d = 2 * WIN_ROWS - 2
    row_pairs = jnp.stack([rpb[:, 0:n_d, :], rpb[:, 1:n_d + 1, :]], axis=2).astype(jnp.float32)
    t = jnp.einsum("hdn,ncl->hdcl", row_pairs.reshape(N_HEADS, n_d, 2 * n_off), onehot2,
                   precision=lax.Precision.HIGHEST)
    t = t * LOG2E + mask
    return t.reshape(PAIRS, 2, n_d, GRID_W, 2 * GRID_W)


def kernel(x_prompt, x_sample, w_in, conv_w, rpb, g_conv, g_attn, w_out, ln_g, ln_b):
    cw = CONV_WIDTH
    y_prompt, y_sample = x_prompt, x_sample
    for d in range(DEPTH):
        w = w_in[d].astype(jnp.bfloat16)
        w_uc = jnp.concatenate([w[:, 0:cw], w[:, 2 * cw:3 * cw]], axis=1)
        w_seg = jnp.stack([w[:, c * cw:(c + 1) * cw] for c in (1, 3, 4, 5, 6, 7)])
        wo = (w_out[d] * (1.0 / DN_ALPHA)).astype(jnp.bfloat16)
        args = (w_uc, w_seg, conv_w[d], g_conv[d].reshape(1, -1), _band_bias(rpb[d]),
                g_attn[d].reshape(1, -1), wo[:cw], wo[cw:], ln_g[d].reshape(1, -1), ln_b[d].reshape(1, -1))
        y_prompt = _layer_call(y_prompt, *args)
        y_sample = _layer_call(y_sample, *args)
    return (y_prompt, y_sample)
```

```python
import functools

import numpy as np
import jax
import jax.numpy as jnp
from jax import lax
from jax.experimental import pallas as pl
from jax.experimental.pallas import tpu as pltpu

D_MODEL = 1024
GRID_W = 64
CONV_WIDTH = 512
CONV_GROUPS = 8
N_HEADS = 8
HEAD_DIM = 64
ATTN_WIDTH = N_HEADS * HEAD_DIM
WIN_ROWS = 8
WIN_COLS = 16
NORM_EPS = 1e-5
DEPTH = 1
DN_ALPHA = (2.0 * DEPTH) ** 0.25
LOG2E = 1.4426950408889634

LANES = 128
GROUP_W = 64
PAIRS = ATTN_WIDTH // LANES
BAND = WIN_ROWS * GRID_W
HALO = 16

CHUNK_ROWS = 16
CHUNK = CHUNK_ROWS * GRID_W
KV_HALO_ROWS = WIN_ROWS // 2
KV_HALO = KV_HALO_ROWS * GRID_W
PROJ_TM = 512
FILL_TM = 256
GROUP_ROWS = 2
GROUP_TASKS = GROUP_ROWS * PAIRS
VMEM_LIMIT = 58 * 1024 * 1024


def _group_mean_square(y):
    rows, width = y.shape
    low = lax.broadcasted_iota(jnp.int32, (rows, LANES), 1) < GROUP_W
    out = []
    for c in range(width // LANES):
        sq = y[:, c * LANES:(c + 1) * LANES]
        sq = sq * sq
        s_lo = jnp.sum(jnp.where(low, sq, 0.0), axis=-1, keepdims=True)
        s_hi = jnp.sum(jnp.where(low, 0.0, sq), axis=-1, keepdims=True)
        out.append(jnp.where(low, s_lo, s_hi))
    return jnp.concatenate(out, axis=-1) * (1.0 / GROUP_W)


def _silu(z):
    return z * jax.nn.sigmoid(z)


def _layer_kernel(xm_ref, xcp_ref, xnr_ref, wuc_ref, wseg_ref, convw_ref, gconv_ref,
                  bias_ref, gattn_ref, wout_ref, lng_ref, lnb_ref,
                  o_ref,
                  xh_s, u_s, q_s, ga_s, yc_s, ya_s, kbuf, vbuf, pbuf0, pbuf1, *, rows):
    i = pl.program_id(1)
    n = pl.num_programs(1)
    cw = CONV_WIDTH
    r0 = i * CHUNK_ROWS

    def store_pairs(buf, start, val):
        val = val.astype(jnp.bfloat16)
        for p in range(PAIRS):
            buf[p, pl.ds(start, val.shape[0]), :] = val[:, p * LANES:(p + 1) * LANES]

    xh_s[0:HALO, :] = jnp.where(i > 0, xcp_ref[0], 0.0).astype(jnp.bfloat16)
    xh_s[HALO:HALO + CHUNK, :] = xm_ref[0].astype(jnp.bfloat16)
    xh_s[HALO + CHUNK:, :] = jnp.where(i < n - 1, xnr_ref[0], 0.0).astype(jnp.bfloat16)

    @pl.when(i > 0)
    def _():
        kbuf[:, 0:2 * KV_HALO, :] = kbuf[:, CHUNK:CHUNK + 2 * KV_HALO, :]
        vbuf[:, 0:2 * KV_HALO, :] = vbuf[:, CHUNK:CHUNK + 2 * KV_HALO, :]

    @pl.when(i == 0)
    def _():
        x0 = xh_s[pl.ds(HALO, KV_HALO), :]
        store_pairs(kbuf, KV_HALO, jnp.dot(x0, wseg_ref[3], preferred_element_type=jnp.float32))
        store_pairs(vbuf, KV_HALO, jnp.dot(x0, wseg_ref[4], preferred_element_type=jnp.float32))

    def u_piece(s):
        half = (CHUNK + 2 * HALO) // 2
        xc_cc = jnp.dot(xh_s[pl.ds(s * half, half), :], wuc_ref[...], preferred_element_type=jnp.float32)
        u_s[pl.ds(s * half, half), :] = xc_cc[:, :cw] * xc_cc[:, cw:]

    def seg(start, c):
        xm = xh_s[pl.ds(HALO + start, PROJ_TM), :]
        return jnp.dot(xm, wseg_ref[c], preferred_element_type=jnp.float32)

    def q_piece(s):
        q = seg(s * PROJ_TM, 2) * (HEAD_DIM ** -0.5 * LOG2E)
        q_s[pl.ds(s * PROJ_TM, PROJ_TM), :] = q.astype(jnp.bfloat16)

    def k_piece(s):
        store_pairs(kbuf, 2 * KV_HALO + s * PROJ_TM, seg(KV_HALO + s * PROJ_TM, 3))

    def v_piece(s):
        store_pairs(vbuf, 2 * KV_HALO + s * PROJ_TM, seg(KV_HALO + s * PROJ_TM, 4))

    def gate_piece(s):
        ga_s[pl.ds(s * PROJ_TM, PROJ_TM), :] = _silu(seg(s * PROJ_TM, 5)).astype(jnp.bfloat16)

    def conv_piece(c):
        t0 = c * FILL_TM
        xm = xh_s[pl.ds(HALO + t0, FILL_TM), :]
        ext = FILL_TM + 2 * HALO
        u_ext = u_s[pl.ds(t0, ext), :]
        u_prev = pltpu.roll(u_ext, 1, 0)[HALO:HALO + FILL_TM]
        u_mid = u_ext[HALO:HALO + FILL_TM]
        u_next = pltpu.roll(u_ext, ext - 1, 0)[HALO:HALO + FILL_TM]
        conv = u_prev * convw_ref[0:1, :] + u_mid * convw_ref[1:2, :] + u_next * convw_ref[2:3, :]
        y = jnp.dot(xm, wseg_ref[0], preferred_element_type=jnp.float32) * conv
        y = y * lax.rsqrt(_group_mean_square(y) + NORM_EPS) * gconv_ref[...]
        z = jnp.dot(xm, wseg_ref[1], preferred_element_type=jnp.float32)
        yc_s[pl.ds(t0, FILL_TM), :] = (y * _silu(z)).astype(jnp.bfloat16)

    def out_piece(c):
        tok = pl.ds(c * FILL_TM, FILL_TM)
        mix = jnp.concatenate([yc_s[tok, :], ya_s[tok, :]], axis=1)
        out = jnp.dot(mix, wout_ref[...], preferred_element_type=jnp.float32)
        h = xm_ref[0, tok, :] + out
        mu = jnp.mean(h, axis=-1, keepdims=True)
        hc = h - mu
        var = jnp.mean(hc * hc, axis=-1, keepdims=True)
        o_ref[0, tok, :] = hc * lax.rsqrt(var + NORM_EPS / DN_ALPHA ** 2) * lng_ref[...] + lnb_ref[...]

    low = lax.broadcasted_iota(jnp.int32, (GRID_W, LANES), 1) < GROUP_W
    ones = jnp.ones((BAND, LANES), jnp.bfloat16)

    def task(g, t):
        rl = g * GROUP_ROWS + t // PAIRS
        r = r0 + rl
        rs = jnp.clip(r - WIN_ROWS // 2, 0, rows - WIN_ROWS)
        off = pl.multiple_of((rs - r0 + KV_HALO_ROWS) * GRID_W, GRID_W)
        qoff = pl.multiple_of(rl * GRID_W, GRID_W)
        p = t % PAIRS
        return off, r - rs, qoff, p, slice(p * LANES, (p + 1) * LANES)

    def scores(g, t, pbuf):
        off, variant, qoff, p, lanes = task(g, t)
        qp = q_s[pl.ds(qoff, GRID_W), lanes]
        zero = jnp.zeros_like(qp)
        qm = jnp.concatenate([jnp.where(low, qp, zero), jnp.where(low, zero, qp)], axis=0)
        kp = kbuf[p, pl.ds(off, BAND), :]
        s = lax.dot_general(qm, kp, (((1,), (1,)), ((), ())),
                            preferred_element_type=jnp.float32)
        bias = jnp.concatenate(
            [jnp.concatenate([bias_ref[p, a, 2 * j2 + (WIN_ROWS - 1) - variant]
                              for j2 in range(WIN_ROWS // 2)], axis=1)
             for a in range(2)], axis=0)
        s = (s + bias).astype(jnp.bfloat16)
        m = jnp.max(s, axis=-1, keepdims=True)
        pbuf[t] = jnp.exp2(s - m)

    def outputs(g, t, pbuf):
        off, _, qoff, p, lanes = task(g, t)
        vp = jnp.concatenate([vbuf[p, pl.ds(off, BAND), :], ones], axis=1)
        o2 = jnp.dot(pbuf[t], vp, preferred_element_type=jnp.float32)
        num = jnp.where(low, o2[:GRID_W, :LANES], o2[GRID_W:, :LANES])
        den = jnp.where(low, o2[:GRID_W, LANES:], o2[GRID_W:, LANES:])
        o = num / den
        o = o * lax.rsqrt(_group_mean_square(o) + NORM_EPS) * gattn_ref[:, lanes]
        gate = ga_s[pl.ds(qoff, GRID_W), lanes].astype(jnp.float32)
        ya_s[pl.ds(qoff, GRID_W), lanes] = (o * gate).astype(jnp.bfloat16)

    n_groups = CHUNK_ROWS // GROUP_ROWS
    pbufs = (pbuf0, pbuf1)

    def group_step(g):
        for t in range(GROUP_TASKS):
            scores(g, t, pbufs[g % 2])
            if g > 0:
                outputs(g - 1, t, pbufs[(g - 1) % 2])

    program = (
        (u_piece, 0), (u_piece, 1),
        (q_piece, 0), (k_piece, 0), (v_piece, 0), (gate_piece, 0),
        (q_piece, 1), (k_piece, 1), (v_piece, 1), (gate_piece, 1),
        (group_step, 0),
        (group_step, 1), (conv_piece, 0),
        (group_step, 2), (conv_piece, 1),
        (group_step, 3), (out_piece, 0),
        (group_step, 4), (conv_piece, 2),
        (group_step, 5), (out_piece, 1),
        (group_step, 6), (conv_piece, 3),
        (group_step, 7), (out_piece, 2),
    )
    for piece, c in program:
        piece(c)
    for t in range(GROUP_TASKS):
        outputs(n_groups - 1, t, pbufs[(n_groups - 1) % 2])
    out_piece(3)


def _layer_call(x, w_uc, w_seg, conv_w, g_conv, bias, g_attn, w_out, ln_g, ln_b):
    bsz, seqlen, d = x.shape
    rows = seqlen // GRID_W
    assert rows % CHUNK_ROWS == 0 and CHUNK // FILL_TM == 4 and CHUNK_ROWS // GROUP_ROWS == 8
    nc = rows // CHUNK_ROWS
    n_kv_halo = seqlen // KV_HALO

    def const(a):
        return pl.BlockSpec(a.shape, lambda b, i: (0,) * a.ndim, pipeline_mode=pl.Buffered(1))

    chunk = pl.BlockSpec((1, CHUNK, d), lambda b, i: (b, i, 0))
    return pl.pallas_call(
        functools.partial(_layer_kernel, rows=rows),
        grid=(bsz, nc),
        in_specs=[
            chunk,
            pl.BlockSpec((1, HALO, d), lambda b, i: (b, jnp.maximum(i * (CHUNK // HALO) - 1, 0), 0)),
            pl.BlockSpec((1, KV_HALO, d),
                         lambda b, i: (b, jnp.minimum((i + 1) * (CHUNK // KV_HALO), n_kv_halo - 1), 0)),
            const(w_uc), const(w_seg), const(conv_w), const(g_conv),
            const(bias), const(g_attn), const(w_out), const(ln_g), const(ln_b),
        ],
        out_specs=chunk,
        out_shape=jax.ShapeDtypeStruct(x.shape, jnp.float32),
        scratch_shapes=[
            pltpu.VMEM((HALO + CHUNK + KV_HALO, d), jnp.bfloat16),
            pltpu.VMEM((CHUNK + 2 * HALO, CONV_WIDTH), jnp.float32),
            pltpu.VMEM((CHUNK, ATTN_WIDTH), jnp.bfloat16),
            pltpu.VMEM((CHUNK, ATTN_WIDTH), jnp.bfloat16),
            pltpu.VMEM((CHUNK, CONV_WIDTH), jnp.bfloat16),
            pltpu.VMEM((CHUNK, ATTN_WIDTH), jnp.bfloat16),
            pltpu.VMEM((PAIRS, CHUNK + 2 * KV_HALO, LANES), jnp.bfloat16),
            pltpu.VMEM((PAIRS, CHUNK + 2 * KV_HALO, LANES), jnp.bfloat16),
            pltpu.VMEM((GROUP_TASKS, 2 * GRID_W, BAND), jnp.bfloat16),
            pltpu.VMEM((GROUP_TASKS, 2 * GRID_W, BAND), jnp.bfloat16),
        ],
        compiler_params=pltpu.CompilerParams(
            dimension_semantics=("arbitrary", "arbitrary"), vmem_limit_bytes=VMEM_LIMIT),
        name="hybrid_layer",
    )(x, x, x, w_uc, w_seg, conv_w, g_conv, bias, g_attn, w_out, ln_g, ln_b)


def _band_bias(rpb):
    cols = np.arange(GRID_W)
    col_start = np.clip(cols - WIN_COLS // 2, 0, GRID_W - WIN_COLS)
    kc = np.arange(GRID_W)
    valid = (kc[None, :] >= col_start[:, None]) & (kc[None, :] < col_start[:, None] + WIN_COLS)
    dc_idx = kc[None, :] - cols[:, None] + (WIN_COLS - 1)
    onehot = (dc_idx[None] == np.arange(2 * WIN_COLS - 1)[:, None, None]) & valid[None]
    n_off = 2 * WIN_COLS - 1
    onehot2 = np.zeros((2, n_off, GRID_W, 2, GRID_W), np.float32)
    onehot2[0, :, :, 0, :] = onehot
    onehot2[1, :, :, 1, :] = onehot
    onehot2 = onehot2.reshape(2 * n_off, GRID_W, 2 * GRID_W)
    mask = np.tile(np.where(valid, 0.0, -np.inf).astype(np.float32), (1, 2))
    n_d = 2 * WIN_ROWS - 2
    row_pairs = jnp.stack([rpb[:, 0:n_d, :], rpb[:, 1:n_d + 1, :]], axis=2).astype(jnp.float32)
    t = jnp.einsum("hdn,ncl->hdcl", row_pairs.reshape(N_HEADS, n_d, 2 * n_off), onehot2,
                   precision=lax.Precision.HIGHEST)
    t = t * LOG2E + mask
    return t.reshape(PAIRS, 2, n_d, GRID_W, 2 * GRID_W)


def kernel(x_prompt, x_sample, w_in, conv_w, rpb, g_conv, g_attn, w_out, ln_g, ln_b):
    cw = CONV_WIDTH
    y_prompt, y_sample = x_prompt, x_sample
    for d in range(DEPTH):
        w = w_in[d].astype(jnp.bfloat16)
        w_uc = jnp.concatenate([w[:, 0:cw], w[:, 2 * cw:3 * cw]], axis=1)
        w_seg = jnp.stack([w[:, c * cw:(c + 1) * cw] for c in (1, 3, 4, 5, 6, 7)])
        wo = (w_out[d] * (1.0 / DN_ALPHA)).astype(jnp.bfloat16)
        args = (w_uc, w_seg, conv_w[d], g_conv[d].reshape(1, -1), _band_bias(rpb[d]),
                g_attn[d].reshape(1, -1), wo, ln_g[d].reshape(1, -1), ln_b[d].reshape(1, -1))
        y_prompt = _layer_call(y_prompt, *args)
        y_sample = _layer_call(y_sample, *args)
    return (y_prompt, y_sample)
```

```python
import functools

import numpy as np
import jax
import jax.numpy as jnp
from jax import lax
from jax.experimental import pallas as pl
from jax.experimental.pallas import tpu as pltpu

D_MODEL = 1024
GRID_W = 64
CONV_WIDTH = 512
CONV_GROUPS = 8
N_HEADS = 8
HEAD_DIM = 64
ATTN_WIDTH = N_HEADS * HEAD_DIM
WIN_ROWS = 8
WIN_COLS = 16
NORM_EPS = 1e-5
DEPTH = 1
DN_ALPHA = (2.0 * DEPTH) ** 0.25
LOG2E = 1.4426950408889634

LANES = 128
GROUP_W = 64
PAIRS = ATTN_WIDTH // LANES
BAND = WIN_ROWS * GRID_W
HALO = 16

CHUNK_ROWS = 16
CHUNK = CHUNK_ROWS * GRID_W
KV_HALO_ROWS = WIN_ROWS // 2
KV_HALO = KV_HALO_ROWS * GRID_W
PROJ_TM = 512
FILL_TM = 256
GROUP_ROWS = 2
GROUP_TASKS = GROUP_ROWS * PAIRS
VMEM_LIMIT = 58 * 1024 * 1024


def _group_mean_square(y):
    rows, width = y.shape
    low = lax.broadcasted_iota(jnp.int32, (rows, LANES), 1) < GROUP_W
    out = []
    for c in range(width // LANES):
        sq = y[:, c * LANES:(c + 1) * LANES]
        sq = sq * sq
        s_lo = jnp.sum(jnp.where(low, sq, 0.0), axis=-1, keepdims=True)
        s_hi = jnp.sum(jnp.where(low, 0.0, sq), axis=-1, keepdims=True)
        out.append(jnp.where(low, s_lo, s_hi))
    return jnp.concatenate(out, axis=-1) * (1.0 / GROUP_W)


def _silu(z):
    return z * jax.nn.sigmoid(z)


def _layer_kernel(xm_ref, xcp_ref, xnr_ref, wuc_ref, wseg_ref, convw_ref, gconv_ref,
                  bias_ref, gattn_ref, wtop_ref, wbot_ref, lng_ref, lnb_ref,
                  o_ref,
                  xh_s, u_s, q_s, ga_s, yc_s, ya_s, kbuf, vbuf, pbuf0, pbuf1, *, rows):
    i = pl.program_id(1)
    n = pl.num_programs(1)
    cw = CONV_WIDTH
    r0 = i * CHUNK_ROWS

    def store_pairs(buf, start, val):
        val = val.astype(jnp.bfloat16)
        for p in range(PAIRS):
            buf[p, pl.ds(start, val.shape[0]), :] = val[:, p * LANES:(p + 1) * LANES]

    xh_s[0:HALO, :] = jnp.where(i > 0, xcp_ref[0], 0.0).astype(jnp.bfloat16)
    xh_s[HALO:HALO + CHUNK, :] = xm_ref[0].astype(jnp.bfloat16)
    xh_s[HALO + CHUNK:, :] = jnp.where(i < n - 1, xnr_ref[0], 0.0).astype(jnp.bfloat16)

    @pl.when(i > 0)
    def _():
        kbuf[:, 0:2 * KV_HALO, :] = kbuf[:, CHUNK:CHUNK + 2 * KV_HALO, :]
        vbuf[:, 0:2 * KV_HALO, :] = vbuf[:, CHUNK:CHUNK + 2 * KV_HALO, :]

    @pl.when(i == 0)
    def _():
        x0 = xh_s[pl.ds(HALO, KV_HALO), :]
        store_pairs(kbuf, KV_HALO, jnp.dot(x0, wseg_ref[3], preferred_element_type=jnp.float32))
        store_pairs(vbuf, KV_HALO, jnp.dot(x0, wseg_ref[4], preferred_element_type=jnp.float32))

    def u_piece(s):
        half = (CHUNK + 2 * HALO) // 2
        xc_cc = jnp.dot(xh_s[pl.ds(s * half, half), :], wuc_ref[...], preferred_element_type=jnp.float32)
        u_s[pl.ds(s * half, half), :] = xc_cc[:, :cw] * xc_cc[:, cw:]

    def seg(start, c):
        xm = xh_s[pl.ds(HALO + start, PROJ_TM), :]
        return jnp.dot(xm, wseg_ref[c], preferred_element_type=jnp.float32)

    def q_piece(s):
        q = seg(s * PROJ_TM, 2) * (HEAD_DIM ** -0.5 * LOG2E)
        q_s[pl.ds(s * PROJ_TM, PROJ_TM), :] = q.astype(jnp.bfloat16)

    def k_piece(s):
        store_pairs(kbuf, 2 * KV_HALO + s * PROJ_TM, seg(KV_HALO + s * PROJ_TM, 3))

    def v_piece(s):
        store_pairs(vbuf, 2 * KV_HALO + s * PROJ_TM, seg(KV_HALO + s * PROJ_TM, 4))

    def gate_piece(s):
        ga_s[pl.ds(s * PROJ_TM, PROJ_TM), :] = _silu(seg(s * PROJ_TM, 5)).astype(jnp.bfloat16)

    def conv_piece(c):
        t0 = c * FILL_TM
        xm = xh_s[pl.ds(HALO + t0, FILL_TM), :]
        ext = FILL_TM + 2 * HALO
        u_ext = u_s[pl.ds(t0, ext), :]
        u_prev = pltpu.roll(u_ext, 1, 0)[HALO:HALO + FILL_TM]
        u_mid = u_ext[HALO:HALO + FILL_TM]
        u_next = pltpu.roll(u_ext, ext - 1, 0)[HALO:HALO + FILL_TM]
        conv = u_prev * convw_ref[0:1, :] + u_mid * convw_ref[1:2, :] + u_next * convw_ref[2:3, :]
        y = jnp.dot(xm, wseg_ref[0], preferred_element_type=jnp.float32) * conv
        y = y * lax.rsqrt(_group_mean_square(y) + NORM_EPS) * gconv_ref[...]
        z = jnp.dot(xm, wseg_ref[1], preferred_element_type=jnp.float32)
        yc_s[pl.ds(t0, FILL_TM), :] = (y * _silu(z)).astype(jnp.bfloat16)

    def out_piece(c):
        tok = pl.ds(c * FILL_TM, FILL_TM)
        out = jnp.dot(yc_s[tok, :], wtop_ref[...], preferred_element_type=jnp.float32)
        out = out + jnp.dot(ya_s[tok, :], wbot_ref[...], preferred_element_type=jnp.float32)
        for r in range(FILL_TM // GRID_W):
            rows_r = pl.ds(c * FILL_TM + r * GRID_W, GRID_W)
            h = xm_ref[0, rows_r, :] + out[r * GRID_W:(r + 1) * GRID_W, :]
            mu = jnp.mean(h, axis=-1, keepdims=True)
            hc = h - mu
            var = jnp.mean(hc * hc, axis=-1, keepdims=True)
            o_ref[0, rows_r, :] = hc * lax.rsqrt(var + NORM_EPS / DN_ALPHA ** 2) * lng_ref[...] + lnb_ref[...]

    low = lax.broadcasted_iota(jnp.int32, (GRID_W, LANES), 1) < GROUP_W
    ones = jnp.ones((BAND, LANES), jnp.bfloat16)

    def task(g, t):
        rl = g * GROUP_ROWS + t // PAIRS
        r = r0 + rl
        rs = jnp.clip(r - WIN_ROWS // 2, 0, rows - WIN_ROWS)
        off = pl.multiple_of((rs - r0 + KV_HALO_ROWS) * GRID_W, GRID_W)
        qoff = pl.multiple_of(rl * GRID_W, GRID_W)
        p = t % PAIRS
        return off, r - rs, qoff, p, slice(p * LANES, (p + 1) * LANES)

    def scores(g, t, pbuf):
        off, variant, qoff, p, lanes = task(g, t)
        qp = q_s[pl.ds(qoff, GRID_W), lanes]
        zero = jnp.zeros_like(qp)
        qm = jnp.concatenate([jnp.where(low, qp, zero), jnp.where(low, zero, qp)], axis=0)
        kp = kbuf[p, pl.ds(off, BAND), :]
        s = lax.dot_general(qm, kp, (((1,), (1,)), ((), ())),
                            preferred_element_type=jnp.float32)
        for a in range(2):
            half = slice(a * GRID_W, (a + 1) * GRID_W)
            bias = jnp.concatenate([bias_ref[p, a, 2 * j2 + (WIN_ROWS - 1) - variant]
                                    for j2 in range(WIN_ROWS // 2)], axis=1)
            sa = (s[half, :] + bias).astype(jnp.bfloat16)
            m = jnp.max(sa, axis=-1, keepdims=True)
            pbuf[t, half, :] = jnp.exp2(sa - m)

    def outputs(g, t, pbuf):
        off, _, qoff, p, lanes = task(g, t)
        vp = jnp.concatenate([vbuf[p, pl.ds(off, BAND), :], ones], axis=1)
        o2 = jnp.dot(pbuf[t], vp, preferred_element_type=jnp.float32)
        num = jnp.where(low, o2[:GRID_W, :LANES], o2[GRID_W:, :LANES])
        den = jnp.where(low, o2[:GRID_W, LANES:], o2[GRID_W:, LANES:])
        o = num / den
        o = o * lax.rsqrt(_group_mean_square(o) + NORM_EPS) * gattn_ref[:, lanes]
        gate = ga_s[pl.ds(qoff, GRID_W), lanes].astype(jnp.float32)
        ya_s[pl.ds(qoff, GRID_W), lanes] = (o * gate).astype(jnp.bfloat16)

    n_groups = CHUNK_ROWS // GROUP_ROWS
    pbufs = (pbuf0, pbuf1)

    def group_step(g):
        for t in range(GROUP_TASKS):
            scores(g, t, pbufs[g % 2])
            if g > 0:
                outputs(g - 1, t, pbufs[(g - 1) % 2])

    program = (
        (u_piece, 0), (u_piece, 1),
        (q_piece, 0), (k_piece, 0), (v_piece, 0), (gate_piece, 0),
        (q_piece, 1), (k_piece, 1), (v_piece, 1), (gate_piece, 1),
        (group_step, 0),
        (group_step, 1), (conv_piece, 0),
        (group_step, 2), (conv_piece, 1),
        (group_step, 3), (out_piece, 0),
        (group_step, 4), (conv_piece, 2),
        (group_step, 5), (out_piece, 1),
        (group_step, 6), (conv_piece, 3),
        (group_step, 7), (out_piece, 2),
    )
    for piece, c in program:
        piece(c)
    for t in range(GROUP_TASKS):
        outputs(n_groups - 1, t, pbufs[(n_groups - 1) % 2])
    out_piece(3)


def _layer_call(x, w_uc, w_seg, conv_w, g_conv, bias, g_attn, w_top, w_bot, ln_g, ln_b):
    bsz, seqlen, d = x.shape
    rows = seqlen // GRID_W
    assert rows % CHUNK_ROWS == 0 and CHUNK // FILL_TM == 4 and CHUNK_ROWS // GROUP_ROWS == 8
    nc = rows // CHUNK_ROWS
    n_kv_halo = seqlen // KV_HALO

    def const(a):
        return pl.BlockSpec(a.shape, lambda b, i: (0,) * a.ndim, pipeline_mode=pl.Buffered(1))

    chunk = pl.BlockSpec((1, CHUNK, d), lambda b, i: (b, i, 0))
    return pl.pallas_call(
        functools.partial(_layer_kernel, rows=rows),
        grid=(bsz, nc),
        in_specs=[
            chunk,
            pl.BlockSpec((1, HALO, d), lambda b, i: (b, jnp.maximum(i * (CHUNK // HALO) - 1, 0), 0)),
            pl.BlockSpec((1, KV_HALO, d),
                         lambda b, i: (b, jnp.minimum((i + 1) * (CHUNK // KV_HALO), n_kv_halo - 1), 0)),
            const(w_uc), const(w_seg), const(conv_w), const(g_conv),
            const(bias), const(g_attn), const(w_top), const(w_bot), const(ln_g), const(ln_b),
        ],
        out_specs=chunk,
        out_shape=jax.ShapeDtypeStruct(x.shape, jnp.float32),
        scratch_shapes=[
            pltpu.VMEM((HALO + CHUNK + KV_HALO, d), jnp.bfloat16),
            pltpu.VMEM((CHUNK + 2 * HALO, CONV_WIDTH), jnp.float32),
            pltpu.VMEM((CHUNK, ATTN_WIDTH), jnp.bfloat16),
            pltpu.VMEM((CHUNK, ATTN_WIDTH), jnp.bfloat16),
            pltpu.VMEM((CHUNK, CONV_WIDTH), jnp.bfloat16),
            pltpu.VMEM((CHUNK, ATTN_WIDTH), jnp.bfloat16),
            pltpu.VMEM((PAIRS, CHUNK + 2 * KV_HALO, LANES), jnp.bfloat16),
            pltpu.VMEM((PAIRS, CHUNK + 2 * KV_HALO, LANES), jnp.bfloat16),
            pltpu.VMEM((GROUP_TASKS, 2 * GRID_W, BAND), jnp.bfloat16),
            pltpu.VMEM((GROUP_TASKS, 2 * GRID_W, BAND), jnp.bfloat16),
        ],
        compiler_params=pltpu.CompilerParams(
            dimension_semantics=("arbitrary", "arbitrary"), vmem_limit_bytes=VMEM_LIMIT),
        name="hybrid_layer",
    )(x, x, x, w_uc, w_seg, conv_w, g_conv, bias, g_attn, w_top, w_bot, ln_g, ln_b)


def _band_bias(rpb):
    cols = np.arange(GRID_W)
    col_start = np.clip(cols - WIN_COLS // 2, 0, GRID_W - WIN_COLS)
    kc = np.arange(GRID_W)
    valid = (kc[None, :] >= col_start[:, None]) & (kc[None, :] < col_start[:, None] + WIN_COLS)
    dc_idx = kc[None, :] - cols[:, None] + (WIN_COLS - 1)
    onehot = (dc_idx[None] == np.arange(2 * WIN_COLS - 1)[:, None, None]) & valid[None]
    n_off = 2 * WIN_COLS - 1
    onehot2 = np.zeros((2, n_off, GRID_W, 2, GRID_W), np.float32)
    onehot2[0, :, :, 0, :] = onehot
    onehot2[1, :, :, 1, :] = onehot
    onehot2 = onehot2.reshape(2 * n_off, GRID_W, 2 * GRID_W)
    mask = np.tile(np.where(valid, 0.0, -np.inf).astype(np.float32), (1, 2))
    n_d = 2 * WIN_ROWS - 2
    row_pairs = jnp.stack([rpb[:, 0:n_d, :], rpb[:, 1:n_d + 1, :]], axis=2).astype(jnp.float32)
    t = jnp.einsum("hdn,ncl->hdcl", row_pairs.reshape(N_HEADS, n_d, 2 * n_off), onehot2,
                   precision=lax.Precision.HIGHEST)
    t = t * LOG2E + mask
    return t.reshape(PAIRS, 2, n_d, GRID_W, 2 * GRID_W)


def kernel(x_prompt, x_sample, w_in, conv_w, rpb, g_conv, g_attn, w_out, ln_g, ln_b):
    cw = CONV_WIDTH
    y_prompt, y_sample = x_prompt, x_sample
    for d in range(DEPTH):
        w = w_in[d].astype(jnp.bfloat16)
        w_uc = jnp.concatenate([w[:, 0:cw], w[:, 2 * cw:3 * cw]], axis=1)
        w_seg = jnp.stack([w[:, c * cw:(c + 1) * cw] for c in (1, 3, 4, 5, 6, 7)])
        wo = (w_out[d] * (1.0 / DN_ALPHA)).astype(jnp.bfloat16)
        args = (w_uc, w_seg, conv_w[d], g_conv[d].reshape(1, -1), _band_bias(rpb[d]),
                g_attn[d].reshape(1, -1), wo[:cw], wo[cw:], ln_g[d].reshape(1, -1), ln_b[d].reshape(1, -1))
        y_prompt = _layer_call(y_prompt, *args)
        y_sample = _layer_call(y_sample, *args)
    return (y_prompt, y_sample)
```

```python
import functools

import numpy as np
import jax
import jax.numpy as jnp
from jax import lax
from jax.experimental import pallas as pl
from jax.experimental.pallas import tpu as pltpu

D_MODEL = 1024
GRID_W = 64
CONV_WIDTH = 512
CONV_GROUPS = 8
N_HEADS = 8
HEAD_DIM = 64
ATTN_WIDTH = N_HEADS * HEAD_DIM
WIN_ROWS = 8
WIN_COLS = 16
NORM_EPS = 1e-5
DEPTH = 1
DN_ALPHA = (2.0 * DEPTH) ** 0.25
LOG2E = 1.4426950408889634

LANES = 128
GROUP_W = 64
PAIRS = ATTN_WIDTH // LANES
BAND = WIN_ROWS * GRID_W
HALO = 16

CHUNK_ROWS = 16
CHUNK = CHUNK_ROWS * GRID_W
KV_HALO_ROWS = WIN_ROWS // 2
KV_HALO = KV_HALO_ROWS * GRID_W
PROJ_TM = 512
FILL_TM = 256
GROUP_ROWS = 2
GROUP_TASKS = GROUP_ROWS * PAIRS
VMEM_LIMIT = 58 * 1024 * 1024


def _group_mean_square(y):
    rows, width = y.shape
    low = lax.broadcasted_iota(jnp.int32, (rows, LANES), 1) < GROUP_W
    out = []
    for c in range(width // LANES):
        sq = y[:, c * LANES:(c + 1) * LANES]
        sq = sq * sq
        s_lo = jnp.sum(jnp.where(low, sq, 0.0), axis=-1, keepdims=True)
        s_hi = jnp.sum(jnp.where(low, 0.0, sq), axis=-1, keepdims=True)
        out.append(jnp.where(low, s_lo, s_hi))
    return jnp.concatenate(out, axis=-1) * (1.0 / GROUP_W)


def _silu(z):
    return z * jax.nn.sigmoid(z)


def _layer_kernel(xm_ref, xcp_ref, xnr_ref, wuc_ref, wseg_ref, convw_ref, gconv_ref,
                  bias_ref, gattn_ref, wtop_ref, wbot_ref, lng_ref, lnb_ref,
                  o_ref,
                  xh_s, u_s, q_s, ga_s, yc_s, ya_s, kbuf, vbuf, pbuf0, pbuf1, *, rows):
    i = pl.program_id(1)
    n = pl.num_programs(1)
    cw = CONV_WIDTH
    r0 = i * CHUNK_ROWS

    def store_pairs(buf, start, val):
        val = val.astype(jnp.bfloat16)
        for p in range(PAIRS):
            buf[p, pl.ds(start, val.shape[0]), :] = val[:, p * LANES:(p + 1) * LANES]

    xh_s[0:HALO, :] = jnp.where(i > 0, xcp_ref[0], 0.0).astype(jnp.bfloat16)
    xh_s[HALO:HALO + CHUNK, :] = xm_ref[0].astype(jnp.bfloat16)
    xh_s[HALO + CHUNK:, :] = jnp.where(i < n - 1, xnr_ref[0], 0.0).astype(jnp.bfloat16)

    @pl.when(i > 0)
    def _():
        kbuf[:, 0:2 * KV_HALO, :] = kbuf[:, CHUNK:CHUNK + 2 * KV_HALO, :]
        vbuf[:, 0:2 * KV_HALO, :] = vbuf[:, CHUNK:CHUNK + 2 * KV_HALO, :]

    @pl.when(i == 0)
    def _():
        x0 = xh_s[pl.ds(HALO, KV_HALO), :]
        store_pairs(kbuf, KV_HALO, jnp.dot(x0, wseg_ref[3], preferred_element_type=jnp.float32))
        store_pairs(vbuf, KV_HALO, jnp.dot(x0, wseg_ref[4], preferred_element_type=jnp.float32))

    def u_piece(s):
        half = (CHUNK + 2 * HALO) // 2
        xc_cc = jnp.dot(xh_s[pl.ds(s * half, half), :], wuc_ref[...], preferred_element_type=jnp.float32)
        u_s[pl.ds(s * half, half), :] = xc_cc[:, :cw] * xc_cc[:, cw:]

    def seg(start, c):
        xm = xh_s[pl.ds(HALO + start, PROJ_TM), :]
        return jnp.dot(xm, wseg_ref[c], preferred_element_type=jnp.float32)

    def q_piece(s):
        q = seg(s * PROJ_TM, 2) * (HEAD_DIM ** -0.5 * LOG2E)
        q_s[pl.ds(s * PROJ_TM, PROJ_TM), :] = q.astype(jnp.bfloat16)

    def k_piece(s):
        store_pairs(kbuf, 2 * KV_HALO + s * PROJ_TM, seg(KV_HALO + s * PROJ_TM, 3))

    def v_piece(s):
        store_pairs(vbuf, 2 * KV_HALO + s * PROJ_TM, seg(KV_HALO + s * PROJ_TM, 4))

    def gate_piece(s):
        ga_s[pl.ds(s * PROJ_TM, PROJ_TM), :] = _silu(seg(s * PROJ_TM, 5)).astype(jnp.bfloat16)

    def conv_piece(c):
        t0 = c * FILL_TM
        xm = xh_s[pl.ds(HALO + t0, FILL_TM), :]
        ext = FILL_TM + 2 * HALO
        u_ext = u_s[pl.ds(t0, ext), :]
        u_prev = pltpu.roll(u_ext, 1, 0)[HALO:HALO + FILL_TM]
        u_mid = u_ext[HALO:HALO + FILL_TM]
        u_next = pltpu.roll(u_ext, ext - 1, 0)[HALO:HALO + FILL_TM]
        conv = u_prev * convw_ref[0:1, :] + u_mid * convw_ref[1:2, :] + u_next * convw_ref[2:3, :]
        y = jnp.dot(xm, wseg_ref[0], preferred_element_type=jnp.float32) * conv
        y = y * lax.rsqrt(_group_mean_square(y) + NORM_EPS) * gconv_ref[...]
        z = jnp.dot(xm, wseg_ref[1], preferred_element_type=jnp.float32)
        yc_s[pl.ds(t0, FILL_TM), :] = (y * _silu(z)).astype(jnp.bfloat16)

    def out_piece(c):
        tok = pl.ds(c * FILL_TM, FILL_TM)
        out = jnp.dot(yc_s[tok, :], wtop_ref[...], preferred_element_type=jnp.float32)
        out = out + jnp.dot(ya_s[tok, :], wbot_ref[...], preferred_element_type=jnp.float32)
        h = xm_ref[0, tok, :] + out
        mu = jnp.mean(h, axis=-1, keepdims=True)
        hc = h - mu
        var = jnp.mean(hc * hc, axis=-1, keepdims=True)
        o_ref[0, tok, :] = hc * lax.rsqrt(var + NORM_EPS / DN_ALPHA ** 2) * lng_ref[...] + lnb_ref[...]

    low = lax.broadcasted_iota(jnp.int32, (GRID_W, LANES), 1) < GROUP_W
    ones = jnp.ones((BAND, LANES), jnp.bfloat16)

    def task(g, t):
        rl = g * GROUP_ROWS + t // PAIRS
        r = r0 + rl
        rs = jnp.clip(r - WIN_ROWS // 2, 0, rows - WIN_ROWS)
        off = pl.multiple_of((rs - r0 + KV_HALO_ROWS) * GRID_W, GRID_W)
        qoff = pl.multiple_of(rl * GRID_W, GRID_W)
        p = t % PAIRS
        return off, r - rs, qoff, p, slice(p * LANES, (p + 1) * LANES)

    def scores(g, t, pbuf):
        off, variant, qoff, p, lanes = task(g, t)
        qp = q_s[pl.ds(qoff, GRID_W), lanes]
        zero = jnp.zeros_like(qp)
        qm = jnp.concatenate([jnp.where(low, qp, zero), jnp.where(low, zero, qp)], axis=0)
        kp = kbuf[p, pl.ds(off, BAND), :]
        s = lax.dot_general(qm, kp, (((1,), (1,)), ((), ())),
                            preferred_element_type=jnp.float32)
        bias = jnp.concatenate(
            [jnp.concatenate([bias_ref[p, a, 2 * j2 + (WIN_ROWS - 1) - variant]
                              for j2 in range(WIN_ROWS // 2)], axis=1)
             for a in range(2)], axis=0)
        s = s.astype(jnp.bfloat16) + bias
        m = jnp.max(s, axis=-1, keepdims=True)
        pbuf[t] = jnp.exp2(s - m)

    def outputs(g, t, pbuf):
        off, _, qoff, p, lanes = task(g, t)
        vp = jnp.concatenate([vbuf[p, pl.ds(off, BAND), :], ones], axis=1)
        o2 = jnp.dot(pbuf[t], vp, preferred_element_type=jnp.float32)
        num = jnp.where(low, o2[:GRID_W, :LANES], o2[GRID_W:, :LANES])
        den = jnp.where(low, o2[:GRID_W, LANES:], o2[GRID_W:, LANES:])
        o = num / den
        o = o * lax.rsqrt(_group_mean_square(o) + NORM_EPS) * gattn_ref[:, lanes]
        gate = ga_s[pl.ds(qoff, GRID_W), lanes].astype(jnp.float32)
        ya_s[pl.ds(qoff, GRID_W), lanes] = (o * gate).astype(jnp.bfloat16)

    n_groups = CHUNK_ROWS // GROUP_ROWS
    pbufs = (pbuf0, pbuf1)

    def group_step(g):
        for t in range(GROUP_TASKS):
            scores(g, t, pbufs[g % 2])
            if g > 0:
                outputs(g - 1, t, pbufs[(g - 1) % 2])

    program = (
        (u_piece, 0), (u_piece, 1),
        (q_piece, 0), (k_piece, 0), (v_piece, 0), (gate_piece, 0),
        (q_piece, 1), (k_piece, 1), (v_piece, 1), (gate_piece, 1),
        (group_step, 0),
        (group_step, 1), (conv_piece, 0),
        (group_step, 2), (conv_piece, 1),
        (group_step, 3), (out_piece, 0),
        (group_step, 4), (conv_piece, 2),
        (group_step, 5), (out_piece, 1),
        (group_step, 6), (conv_piece, 3),
        (group_step, 7), (out_piece, 2),
    )
    for piece, c in program:
        piece(c)
    for t in range(GROUP_TASKS):
        outputs(n_groups - 1, t, pbufs[(n_groups - 1) % 2])
    out_piece(3)


def _layer_call(x, w_uc, w_seg, conv_w, g_conv, bias, g_attn, w_top, w_bot, ln_g, ln_b):
    bsz, seqlen, d = x.shape
    rows = seqlen // GRID_W
    assert rows % CHUNK_ROWS == 0 and CHUNK // FILL_TM == 4 and CHUNK_ROWS // GROUP_ROWS == 8
    nc = rows // CHUNK_ROWS
    n_kv_halo = seqlen // KV_HALO

    def const(a):
        return pl.BlockSpec(a.shape, lambda b, i: (0,) * a.ndim, pipeline_mode=pl.Buffered(1))

    chunk = pl.BlockSpec((1, CHUNK, d), lambda b, i: (b, i, 0))
    return pl.pallas_call(
        functools.partial(_layer_kernel, rows=rows),
        grid=(bsz, nc),
        in_specs=[
            chunk,
            pl.BlockSpec((1, HALO, d), lambda b, i: (b, jnp.maximum(i * (CHUNK // HALO) - 1, 0), 0)),
            pl.BlockSpec((1, KV_HALO, d),
                         lambda b, i: (b, jnp.minimum((i + 1) * (CHUNK // KV_HALO), n_kv_halo - 1), 0)),
            const(w_uc), const(w_seg), const(conv_w), const(g_conv),
            const(bias), const(g_attn), const(w_top), const(w_bot), const(ln_g), const(ln_b),
        ],
        out_specs=chunk,
        out_shape=jax.ShapeDtypeStruct(x.shape, jnp.float32),
        scratch_shapes=[
            pltpu.VMEM((HALO + CHUNK + KV_HALO, d), jnp.bfloat16),
            pltpu.VMEM((CHUNK + 2 * HALO, CONV_WIDTH), jnp.float32),
            pltpu.VMEM((CHUNK, ATTN_WIDTH), jnp.bfloat16),
            pltpu.VMEM((CHUNK, ATTN_WIDTH), jnp.bfloat16),
            pltpu.VMEM((CHUNK, CONV_WIDTH), jnp.bfloat16),
            pltpu.VMEM((CHUNK, ATTN_WIDTH), jnp.bfloat16),
            pltpu.VMEM((PAIRS, CHUNK + 2 * KV_HALO, LANES), jnp.bfloat16),
            pltpu.VMEM((PAIRS, CHUNK + 2 * KV_HALO, LANES), jnp.bfloat16),
            pltpu.VMEM((GROUP_TASKS, 2 * GRID_W, BAND), jnp.bfloat16),
            pltpu.VMEM((GROUP_TASKS, 2 * GRID_W, BAND), jnp.bfloat16),
        ],
        compiler_params=pltpu.CompilerParams(
            dimension_semantics=("arbitrary", "arbitrary"), vmem_limit_bytes=VMEM_LIMIT),
        name="hybrid_layer",
    )(x, x, x, w_uc, w_seg, conv_w, g_conv, bias, g_attn, w_top, w_bot, ln_g, ln_b)


def _band_bias(rpb):
    cols = np.arange(GRID_W)
    col_start = np.clip(cols - WIN_COLS // 2, 0, GRID_W - WIN_COLS)
    kc = np.arange(GRID_W)
    valid = (kc[None, :] >= col_start[:, None]) & (kc[None, :] < col_start[:, None] + WIN_COLS)
    dc_idx = kc[None, :] - cols[:, None] + (WIN_COLS - 1)
    onehot = (dc_idx[None] == np.arange(2 * WIN_COLS - 1)[:, None, None]) & valid[None]
    n_off = 2 * WIN_COLS - 1
    onehot2 = np.zeros((2, n_off, GRID_W, 2, GRID_W), np.float32)
    onehot2[0, :, :, 0, :] = onehot
    onehot2[1, :, :, 1, :] = onehot
    onehot2 = onehot2.reshape(2 * n_off, GRID_W, 2 * GRID_W)
    mask = np.tile(np.where(valid, 0.0, -np.inf).astype(np.float32), (1, 2))
    n_d = 2 * WIN_ROWS - 2
    row_pairs = jnp.stack([rpb[:, 0:n_d, :], rpb[:, 1:n_d + 1, :]], axis=2).astype(jnp.float32)
    t = jnp.einsum("hdn,ncl->hdcl", row_pairs.reshape(N_HEADS, n_d, 2 * n_off), onehot2,
                   precision=lax.Precision.HIGHEST)
    t = t * LOG2E + mask
    return t.reshape(PAIRS, 2, n_d, GRID_W, 2 * GRID_W).astype(jnp.bfloat16)


def kernel(x_prompt, x_sample, w_in, conv_w, rpb, g_conv, g_attn, w_out, ln_g, ln_b):
    cw = CONV_WIDTH
    y_prompt, y_sample = x_prompt, x_sample
    for d in range(DEPTH):
        w = w_in[d].astype(jnp.bfloat16)
        w_uc = jnp.concatenate([w[:, 0:cw], w[:, 2 * cw:3 * cw]], axis=1)
        w_seg = jnp.stack([w[:, c * cw:(c + 1) * cw] for c in (1, 3, 4, 5, 6, 7)])
        wo = (w_out[d] * (1.0 / DN_ALPHA)).astype(jnp.bfloat16)
        args = (w_uc, w_seg, conv_w[d], g_conv[d].reshape(1, -1), _band_bias(rpb[d]),
                g_attn[d].reshape(1, -1), wo[:cw], wo[cw:], ln_g[d].reshape(1, -1), ln_b[d].reshape(1, -1))
        y_prompt = _layer_call(y_prompt, *args)
        y_sample = _layer_call(y_sample, *args)
    return (y_prompt, y_sample)
```
